```python
import math
import jax
import jax.numpy as jnp
from jax import lax
import numpy as np

D_MODEL = 1024
BATCH = 16
SEQ = 2048
DEPTH = 4

GRID_W = 64
CTX_LEN = 256
N_BRANCH = 4
BR_WIDTH = D_MODEL // 4
DA_HEADS = 4
DA_DV = BR_WIDTH // DA_HEADS
DA_DH = DA_DV // 2
NA_HEADS = 4
NA_DH = BR_WIDTH // NA_HEADS
NA_WIN_H = 8
NA_WIN_W = 16
HG_HEADS = 4
HG_DK = BR_WIDTH // HG_HEADS
HG_DV = BR_WIDTH // HG_HEADS
HG_CHUNK = 64
LB_FLOOR = 1e-20
FT_GROUPS = 4
FT_DG = BR_WIDTH // FT_GROUPS
SPLIT_IDX = [3 * BR_WIDTH, 6 * BR_WIDTH, 10 * BR_WIDTH, 11 * BR_WIDTH]
IN_WIDTH = 15 * BR_WIDTH
Q_BLOCK = 128
ROPE_THETA = 10000.0
EPS = 1e-6
NEG_INF = -1e30

kernel_name = 'hybrid_diffusion_parallel_mixers'


def rmsnorm(x, gain):
    xf = x.astype(jnp.float32)
    y = xf * lax.rsqrt(jnp.mean(xf * xf, axis=-1, keepdims=True) + EPS)
    return (y * gain.astype(jnp.float32)).astype(x.dtype)


def axial_rope(n, dim):
    t = jnp.arange(n)
    row = (t // GRID_W).astype(jnp.float32)
    col = (t % GRID_W).astype(jnp.float32)
    d_axis = dim // 2
    inv = ROPE_THETA ** (-jnp.arange(0, d_axis, 2, dtype=jnp.float32) / d_axis)
    ang = jnp.concatenate([row[:, None] * inv, col[:, None] * inv], axis=-1)
    return jnp.cos(ang), jnp.sin(ang)


def apply_rope(x, cos, sin):
    x1, x2 = jnp.split(x, 2, axis=-1)
    c = cos[None, :, None, :].astype(x.dtype)
    s = sin[None, :, None, :].astype(x.dtype)
    return jnp.concatenate([x1 * c - x2 * s, x1 * s + x2 * c], axis=-1)


def softmax_attention(q, k, v):
    s = jnp.einsum('bqhd,bkhd->bhqk', q, k, preferred_element_type=jnp.float32) * (q.shape[-1] ** -0.5)
    p = jax.nn.softmax(s, axis=-1).astype(v.dtype)
    return jnp.einsum('bhqk,bkhd->bqhd', p, v)


def diff_softmax_attention(q, k, v, lam):
    b, nq = q.shape[:2]
    nk = k.shape[1]
    nb = nq // Q_BLOCK
    scale = DA_DH ** -0.5
    qb = q.reshape(b, nb, Q_BLOCK, 2 * DA_HEADS, DA_DH).swapaxes(0, 1)

    def block(qi):
        s = jnp.einsum('bqhd,bkhd->bhqk', qi, k, preferred_element_type=jnp.float32) * scale
        p = jax.nn.softmax(s, axis=-1).reshape(b, DA_HEADS, 2, Q_BLOCK, nk)
        a = p[:, :, 0] - lam * p[:, :, 1]
        return jnp.einsum('bhqk,bkhe->bqhe', a.astype(v.dtype), v)

    o = lax.map(block, qb)
    return o.swapaxes(0, 1).reshape(b, nq, DA_HEADS, DA_DV)


def mixer_diff_attn(z, zc, qk_gain, lam_vec, subln_gain, lam_init, cos, sin, need_ctx):
    def project(t):
        b, n, _ = t.shape
        q, k, v = jnp.split(t, 3, axis=-1)
        q = rmsnorm(q.reshape(b, n, 2 * DA_HEADS, DA_DH), qk_gain[0])
        k = rmsnorm(k.reshape(b, n, 2 * DA_HEADS, DA_DH), qk_gain[1])
        return q, k, v.reshape(b, n, DA_HEADS, DA_DV)

    lv = lam_vec.astype(jnp.float32)
    lam = jnp.exp(jnp.sum(lv[0] * lv[1])) - jnp.exp(jnp.sum(lv[2] * lv[3])) + lam_init
    q, k, v = project(z)
    q = apply_rope(q, cos, sin)
    k = apply_rope(k, cos, sin)
    qc, kc, vc = project(zc)

    def finish(o):
        b, n = o.shape[:2]
        return (rmsnorm(o, subln_gain) * (1.0 - lam_init)).reshape(b, n, BR_WIDTH)

    y = finish(diff_softmax_attention(q, jnp.concatenate([k, kc], axis=1), jnp.concatenate([v, vc], axis=1), lam))
    yc = finish(diff_softmax_attention(qc, kc, vc, lam)) if need_ctx else None
    return y, yc


def neighbourhood_attention(q, k, v, kc, vc, rpb):
    b, n, h, dh = q.shape
    rows = n // GRID_W
    wh = min(NA_WIN_H, rows)
    ww = NA_WIN_W
    scale = dh ** -0.5
    r = jnp.arange(rows)
    col = jnp.arange(GRID_W)
    r0 = jnp.clip(r - wh // 2, 0, rows - wh)
    key_rows = r0[:, None] + jnp.arange(wh)[None, :]
    c0 = jnp.clip(col - ww // 2, 0, GRID_W - ww)
    in_win = (col[None, :] >= c0[:, None]) & (col[None, :] < c0[:, None] + ww)
    dr = key_rows - r[:, None] + NA_WIN_H - 1
    dc = jnp.clip(col[None, :] - col[:, None], 1 - ww, ww - 1) + ww - 1
    bias = rpb.astype(jnp.float32)[:, dr[:, :, None, None], dc[None, None, :, :]]
    bias = jnp.where(in_win[None, None, None], bias, NEG_INF).transpose(1, 3, 0, 2, 4)
    qg = q.reshape(b, rows, GRID_W, h, dh)
    kg = k.reshape(b, rows, GRID_W, h, dh)[:, key_rows]
    vg = v.reshape(b, rows, GRID_W, h, dh)[:, key_rows]
    s_win = jnp.einsum('brqhd,brjkhd->brqhjk', qg, kg, preferred_element_type=jnp.float32) * scale + bias
    s_ctx = jnp.einsum('brqhd,blhd->brqhl', qg, kc, preferred_element_type=jnp.float32) * scale
    nw = wh * GRID_W
    s = jnp.concatenate([s_win.reshape(b, rows, GRID_W, h, nw), s_ctx], axis=-1)
    p = jax.nn.softmax(s, axis=-1).astype(v.dtype)
    p_win = p[..., :nw].reshape(b, rows, GRID_W, h, wh, GRID_W)
    o = jnp.einsum('brqhjk,brjkhd->brqhd', p_win, vg) + jnp.einsum('brqhl,blhd->brqhd', p[..., nw:], vc)
    return o.reshape(b, n, h, dh)


def mixer_neigh_attn(z, zc, qk_gain, rpb, need_ctx):
    def project(t):
        b, n, _ = t.shape
        q, k, v = [u.reshape(b, n, NA_HEADS, NA_DH) for u in jnp.split(t, 3, axis=-1)]
        return rmsnorm(q, qk_gain[0]), rmsnorm(k, qk_gain[1]), v

    b, n, _ = z.shape
    q, k, v = project(z)
    qc, kc, vc = project(zc)
    y = neighbourhood_attention(q, k, v, kc, vc, rpb).reshape(b, n, BR_WIDTH)
    yc = softmax_attention(qc, kc, vc).reshape(b, zc.shape[1], BR_WIDTH) if need_ctx else None
    return y, yc


def hgrn2_scan(q, v, log_f, s0, with_output=True):
    b, n, h, _ = q.shape
    nc = n // HG_CHUNK

    def chunks(t):
        return t.reshape(b, nc, HG_CHUNK, h, t.shape[-1]).transpose(1, 0, 3, 2, 4)

    causal = jnp.tril(jnp.ones((HG_CHUNK, HG_CHUNK), dtype=bool))[:, :, None]

    def step(S, inp):
        qi, vi, gi = inp
        bcum = jnp.cumsum(gi, axis=2)
        ki = -jnp.expm1(gi)
        b_last = bcum[:, :, -1:, :]
        S_new = jnp.exp(b_last)[:, :, 0, :, None] * S + jnp.einsum('bhsd,bhse->bhde', ki * jnp.exp(b_last - bcum), vi)
        if not with_output:
            return S_new, None
        rel = bcum[:, :, :, None, :] - bcum[:, :, None, :, :]
        decay = jnp.where(causal, jnp.exp(jnp.minimum(rel, 0.0)), 0.0)
        att = jnp.einsum('bhtd,bhsd,bhtsd->bhts', qi, ki, decay)
        o = jnp.einsum('bhts,bhse->bhte', att, vi) + jnp.einsum('bhtd,bhde->bhte', qi * jnp.exp(bcum), S)
        return S_new, o

    s_fin, o = lax.scan(step, s0, (chunks(q), chunks(v), chunks(log_f)))
    if not with_output:
        return None, s_fin
    return o.transpose(1, 0, 3, 2, 4).reshape(b, n, h, v.shape[-1]), s_fin


def mixer_hgrn2(z, zc, lb, onorm_gain, need_ctx):
    log_lb = jnp.log(jnp.maximum(lb, LB_FLOOR))
    log_1mlb = jnp.log1p(-lb)

    def prep(t):
        b, n, _ = t.shape
        q, ff, fb, i = jnp.split(t.astype(jnp.float32), 4, axis=-1)

        def log_forget(f, d):
            return jnp.logaddexp(log_lb[d], log_1mlb[d] + jax.nn.log_sigmoid(f)).reshape(b, n, HG_HEADS, HG_DK)

        return (jax.nn.silu(q).reshape(b, n, HG_HEADS, HG_DK), log_forget(ff, 0), log_forget(fb, 1),
                i.reshape(b, n, HG_HEADS, HG_DV))

    def flip(t):
        return jnp.flip(t, axis=1)

    b, n, _ = z.shape
    q, g_f, g_b, i = prep(z)
    qc, gc_f, gc_b, ic = prep(zc)
    s0 = jnp.zeros((b, HG_HEADS, HG_DK, HG_DV), jnp.float32)
    oc_f, sc_f = hgrn2_scan(qc, ic, gc_f, s0, need_ctx)
    oc_b, sc_b = hgrn2_scan(flip(qc), flip(ic), flip(gc_b), s0, need_ctx)
    o_f, _ = hgrn2_scan(q, i, g_f, sc_f)
    o_b, _ = hgrn2_scan(flip(q), flip(i), flip(g_b), sc_b)

    def finish(o, m):
        return rmsnorm(o, onorm_gain).reshape(b, m, BR_WIDTH).astype(z.dtype)

    y = finish(o_f + flip(o_b), n)
    yc = finish(oc_f + flip(oc_b), zc.shape[1]) if need_ctx else None
    return y, yc


def fourier_mix(u):
    b, n, _ = u.shape
    ug = u.astype(jnp.float32).reshape(b, n, FT_GROUPS, FT_DG)
    return jnp.fft.fft2(ug, axes=(1, 3), norm='ortho').real.reshape(b, n, BR_WIDTH).astype(u.dtype)


def merge_branches(h, ys, g, w_up_l, w_merge_l, w_out_l):
    gs = jnp.split(g, N_BRANCH, axis=-1)
    acc = None
    for i in range(N_BRANCH):
        branch = (ys[i] * jax.nn.silu(gs[i])) @ w_up_l[i]
        term = jax.nn.sigmoid(h @ w_merge_l[i]) * branch
        acc = term if acc is None else acc + term
    return acc @ w_out_l


def setup_inputs(seed: int = 0) -> dict:
    key = jax.random.key(seed)
    ks = jax.random.split(key, 18)

    def nrm(k, shape, s):
        return jax.random.normal(k, shape, jnp.float32) * s

    return {
        'x': nrm(ks[0], (BATCH, SEQ, D_MODEL), 1.0),
        'c': nrm(ks[1], (BATCH, D_MODEL), 1.0),
        'ctx': nrm(ks[2], (BATCH, CTX_LEN, D_MODEL), 1.0),
        'c_ctx': nrm(ks[3], (D_MODEL,), 1.0),
        'norm_gain': 1.0 + nrm(ks[4], (DEPTH, D_MODEL), 0.02),
        'w_mod': nrm(ks[5], (DEPTH, D_MODEL, 3 * D_MODEL), 0.5 * D_MODEL ** -0.5),
        'b_mod': nrm(ks[6], (DEPTH, 3 * D_MODEL), 0.01),
        'w_in': nrm(ks[7], (DEPTH, D_MODEL, IN_WIDTH), D_MODEL ** -0.5),
        'da_qk_gain': 1.0 + nrm(ks[8], (DEPTH, 2, DA_DH), 0.02),
        'da_lambda': nrm(ks[9], (DEPTH, 4, DA_DH), 0.1),
        'da_subln_gain': 1.0 + nrm(ks[10], (DEPTH, DA_DV), 0.02),
        'na_qk_gain': 1.0 + nrm(ks[11], (DEPTH, 2, NA_DH), 0.02),
        'na_rpb': nrm(ks[12], (DEPTH, NA_HEADS, 2 * NA_WIN_H - 1, 2 * NA_WIN_W - 1), 0.2),
        'hg_lb_logits': nrm(ks[13], (2, DEPTH, BR_WIDTH), 0.5),
        'hg_norm_gain': 1.0 + nrm(ks[14], (DEPTH, HG_DV), 0.02),
        'w_up': nrm(ks[15], (DEPTH, N_BRANCH, BR_WIDTH, D_MODEL), BR_WIDTH ** -0.5),
        'w_merge': nrm(ks[16], (DEPTH, N_BRANCH, D_MODEL, D_MODEL), D_MODEL ** -0.5),
        'w_out': nrm(ks[17], (DEPTH, D_MODEL, D_MODEL), D_MODEL ** -0.5),
    }


def reference(x, c, ctx, c_ctx, norm_gain, w_mod, b_mod, w_in, da_qk_gain, da_lambda, da_subln_gain,
              na_qk_gain, na_rpb, hg_lb_logits, hg_norm_gain, w_up, w_merge, w_out):
    n = x.shape[1]
    cos, sin = axial_rope(n, DA_DH)
    p_lb = jax.nn.softmax(hg_lb_logits.astype(jnp.float32), axis=1)
    lower_bounds = jnp.cumsum(p_lb, axis=1) - p_lb[:, :1]
    silu_c = jax.nn.silu(c)
    silu_cc = jax.nn.silu(c_ctx)
    xc = ctx
    for l in range(DEPTH):
        need_ctx = l < DEPTH - 1
        lam_init = 0.8 - 0.6 * math.exp(-0.3 * l)
        shift, scale, gate = jnp.split((silu_c @ w_mod[l] + b_mod[l])[:, None, :], 3, axis=-1)
        shift_c, scale_c, gate_c = jnp.split(silu_cc @ w_mod[l] + b_mod[l], 3)
        h = rmsnorm(x, norm_gain[l]) * (1.0 + scale) + shift
        hc = rmsnorm(xc, norm_gain[l]) * (1.0 + scale_c) + shift_c
        za, zb, zh, zf, zg = jnp.split(h @ w_in[l], SPLIT_IDX, axis=-1)
        zc = hc @ (w_in[l] if need_ctx else w_in[l][:, :SPLIT_IDX[2]])
        zc_parts = jnp.split(zc, SPLIT_IDX if need_ctx else SPLIT_IDX[:2], axis=-1)
        ya, yca = mixer_diff_attn(za, zc_parts[0], da_qk_gain[l], da_lambda[l], da_subln_gain[l], lam_init, cos, sin, need_ctx)
        yb, ycb = mixer_neigh_attn(zb, zc_parts[1], na_qk_gain[l], na_rpb[l], need_ctx)
        yh, ych = mixer_hgrn2(zh, zc_parts[2], lower_bounds[:, l], hg_norm_gain[l], need_ctx)
        yf = fourier_mix(zf)
        x = x + gate * merge_branches(h, (ya, yb, yh, yf), zg, w_up[l], w_merge[l], w_out[l])
        if need_ctx:
            ycf = fourier_mix(zc_parts[3])
            xc = xc + gate_c * merge_branches(hc, (yca, ycb, ych, ycf), zc_parts[4], w_up[l], w_merge[l], w_out[l])
    return x
```

```python
import functools
import math

import numpy as np
import jax
import jax.numpy as jnp
from jax import lax
from jax.experimental import pallas as pl
from jax.experimental.pallas import tpu as pltpu

D_MODEL = 1024
DEPTH = 4
GRID_W = 64
BR = 256
DA_HEADS, DA_DH, DA_DV = 4, 32, 64
NA_HEADS, NA_DH, NA_WIN_H, NA_WIN_W = 4, 64, 8, 16
HG_HEADS, HG_D, HG_CHUNK = 4, 64, 64
FT_DG = 64
IN_WIDTH = 15 * BR
LB_FLOOR = 1e-20
ROPE_THETA = 10000.0
EPS = 1e-6
NEG_INF = -1e30

COL_SG = 0
COL_QA, COL_KA, COL_VA, COL_QB, COL_KB, COL_VB, COL_HQ, COL_HI, COL_U = 4, 5, 6, 7, 8, 9, 10, 11, 12
SLAB_W = 13 * BR

NA_QROWS = 4
NA_KROWS = 12
MOD_ROWS = 24

VMEM_LIMIT_BYTES = 56 * 1024 * 1024

F32 = jnp.float32
BF16 = jnp.bfloat16


def _cparams(n_axes):
    return pltpu.CompilerParams(dimension_semantics=("arbitrary",) * n_axes, vmem_limit_bytes=VMEM_LIMIT_BYTES)


def _resident(shape):
    nd = len(shape)
    return pl.BlockSpec(shape, lambda *_: (0,) * nd, pipeline_mode=pl.Buffered(1))


def _dot(a, b):
    return jnp.dot(a, b, preferred_element_type=F32)


def _dot_nt(a, b):
    return lax.dot_general(a, b, (((1,), (1,)), ((), ())), preferred_element_type=F32)


def _dot_tn(a, b):
    return lax.dot_general(a, b, (((0,), (0,)), ((), ())), preferred_element_type=F32)


def _split3(v):
    hi = v.astype(BF16)
    r = v - hi.astype(F32)
    mid = r.astype(BF16)
    lo = (r - mid.astype(F32)).astype(BF16)
    return hi, mid, lo


def _group_mean(v, gsize):
    w = v.shape[-1]
    r = lax.broadcasted_iota(jnp.int32, (w, w), 0) // gsize
    c = lax.broadcasted_iota(jnp.int32, (w, w), 1) // gsize
    ones = jnp.where(r == c, 1.0 / gsize, 0.0).astype(BF16)
    hi, mid, lo = _split3(v)
    return _dot(hi, ones) + _dot(mid, ones) + _dot(lo, ones)


def _group_rmsnorm(v, gain, gsize):
    return v * lax.rsqrt(_group_mean(v * v, gsize) + EPS) * gain


def _silu(v):
    return v * jax.nn.sigmoid(v)


def _mod_kernel(c_ref, w_ref, b_ref, o_ref):
    s = _silu(c_ref[...]).astype(BF16)
    o_ref[...] = _dot(s, w_ref[...].astype(BF16)) + b_ref[...]


def _modulation(cc, w_mod, b_mod):
    tn = D_MODEL
    return pl.pallas_call(
        _mod_kernel,
        grid=(DEPTH, 3 * D_MODEL // tn),
        in_specs=[
            pl.BlockSpec((MOD_ROWS, D_MODEL), lambda l, j: (0, 0)),
            pl.BlockSpec((None, D_MODEL, tn), lambda l, j: (l, 0, j)),
            pl.BlockSpec((None, 1, tn), lambda l, j: (l, 0, j)),
        ],
        out_specs=pl.BlockSpec((None, MOD_ROWS, tn), lambda l, j: (l, 0, j)),
        out_shape=jax.ShapeDtypeStruct((DEPTH, MOD_ROWS, 3 * D_MODEL), F32),
        compiler_params=_cparams(2),
        name="modulation",
    )(cc, w_mod, b_mod.reshape(DEPTH, 1, 3 * D_MODEL))


def _modulated_norm(x, mod_ref, ng_ref):
    ms = jnp.mean(x * x, axis=-1, keepdims=True)
    y = x * lax.rsqrt(ms + EPS) * ng_ref[...]
    return y * (1.0 + mod_ref[1:2, :]) + mod_ref[0:1, :]


def _log_forget(f, log_lb, log_1mlb):
    ls = jnp.minimum(f, 0.0) - jnp.log1p(jnp.exp(-jnp.abs(f)))
    a = log_lb
    b = log_1mlb + ls
    return jnp.maximum(a, b) + jnp.log1p(jnp.exp(-jnp.abs(a - b)))


def _rope(v, cos, sin_signed):
    lane = lax.broadcasted_iota(jnp.int32, v.shape, 1)
    first_half = (lane % DA_DH) < (DA_DH // 2)
    swapped = jnp.where(first_half, pltpu.roll(v, BR - DA_DH // 2, 1), pltpu.roll(v, DA_DH // 2, 1))
    return v * cos + swapped * sin_signed


def _inproj_kernel(x_ref, mod_ref, ng_ref, w_ref, gda_ref, gna_ref, lb_ref, cos_ref, sin_ref, slab_ref, g_ref, *, rope):
    h = _modulated_norm(x_ref[...], mod_ref, ng_ref).astype(BF16)

    def proj(col):
        return _dot(h, w_ref[:, col * BR:(col + 1) * BR])

    def put(col, v):
        slab_ref[:, col * BR:(col + 1) * BR] = v.astype(BF16)

    q = _group_rmsnorm(proj(0), gda_ref[0:1, :], DA_DH)
    k = _group_rmsnorm(proj(1), gda_ref[1:2, :], DA_DH)
    if rope:
        q = _rope(q, cos_ref[...], sin_ref[...])
        k = _rope(k, cos_ref[...], sin_ref[...])
    put(COL_QA, q * (DA_DH ** -0.5))
    put(COL_KA, k)
    put(COL_VA, proj(2))
    put(COL_QB, _group_rmsnorm(proj(3), gna_ref[0:1, :], NA_DH) * (NA_DH ** -0.5))
    put(COL_KB, _group_rmsnorm(proj(4), gna_ref[1:2, :], NA_DH))
    put(COL_VB, proj(5))
    put(COL_HQ, _silu(proj(6)))
    g_ref[:, 0:BR] = _log_forget(proj(7), lb_ref[0:1, :], lb_ref[1:2, :])
    g_ref[:, BR:2 * BR] = _log_forget(proj(8), lb_ref[2:3, :], lb_ref[3:4, :])
    put(COL_HI, proj(9))
    put(COL_U, proj(10))
    for i in range(4):
        put(COL_SG + i, _silu(proj(11 + i)))


def _inproj(x2d, mod_l, ng, w, gda, gna, lbp, cos, sin, *, tm, rows_per_mod, mod_row0, rope):
    n = x2d.shape[0]
    n_pos = cos.shape[0]
    tiles_per_seq = n_pos // tm

    def mod_map(t):
        return ((t * tm) // rows_per_mod + mod_row0, 0, 0)

    return pl.pallas_call(
        functools.partial(_inproj_kernel, rope=rope),
        grid=(n // tm,),
        in_specs=[
            pl.BlockSpec((tm, D_MODEL), lambda t: (t, 0)),
            pl.BlockSpec((None, 3, D_MODEL), mod_map),
            _resident((1, D_MODEL)),
            _resident((D_MODEL, IN_WIDTH)),
            _resident((2, BR)),
            _resident((2, BR)),
            _resident((4, BR)),
            pl.BlockSpec((tm, BR), lambda t: (t % tiles_per_seq, 0)),
            pl.BlockSpec((tm, BR), lambda t: (t % tiles_per_seq, 0)),
        ],
        out_specs=[
            pl.BlockSpec((tm, SLAB_W), lambda t: (t, 0)),
            pl.BlockSpec((tm, 2 * BR), lambda t: (t, 0)),
        ],
        out_shape=[
            jax.ShapeDtypeStruct((n, SLAB_W), BF16),
            jax.ShapeDtypeStruct((n, 2 * BR), F32),
        ],
        compiler_params=_cparams(1),
        name="inproj_rope" if rope else "inproj",
    )(x2d, mod_l, ng, w, gda, gna, lbp, cos, sin)


def _softmax_pv(q, segments):
    scores = [_dot_nt(q, k) for k, _ in segments]
    m = scores[0].max(axis=-1, keepdims=True)
    for s in scores[1:]:
        m = jnp.maximum(m, s.max(axis=-1, keepdims=True))
    num, den = None, None
    for s, (_, v) in zip(scores, segments):
        e = jnp.exp(s - m)
        d = e.sum(axis=-1, keepdims=True)
        p = _dot(e.astype(BF16), v)
        num = p if num is None else num + p
        den = d if den is None else den + d
    return num / den


def _da_kernel(lam_ref, g_ref, q_ref, *refs, lam_init, n_seg):
    kv_refs, o_ref = refs[:2 * n_seg], refs[2 * n_seg]
    lam = lam_ref[0]
    outs = []
    for h in range(DA_HEADS):
        vs = slice(h * DA_DV, (h + 1) * DA_DV)
        att = []
        for j in range(2):
            qs = slice((2 * h + j) * DA_DH, (2 * h + j + 1) * DA_DH)
            segs = [(kv_refs[2 * s][:, qs], kv_refs[2 * s + 1][:, vs]) for s in range(n_seg)]
            att.append(_softmax_pv(q_ref[:, qs], segs))
        o = att[0] - lam * att[1]
        o = o * lax.rsqrt(jnp.mean(o * o, axis=-1, keepdims=True) + EPS) * g_ref[...] * (1.0 - lam_init)
        outs.append(o)
    o_ref[...] = jnp.concatenate(outs, axis=-1).astype(BF16)


def _diff_attention(lam, subln, slab_q, key_slabs, *, nq, tq, lam_init):
    batch = slab_q.shape[0] // nq
    qb = nq // tq
    in_specs = [
        pl.BlockSpec(memory_space=pltpu.SMEM),
        _resident((1, DA_DV)),
        pl.BlockSpec((tq, BR), lambda b, i: (b * qb + i, COL_QA)),
    ]
    args = [lam, subln, slab_q]
    for slab, nk in key_slabs:
        in_specs += [pl.BlockSpec((nk, BR), lambda b, i: (b, COL_KA)), pl.BlockSpec((nk, BR), lambda b, i: (b, COL_VA))]
        args += [slab, slab]
    return pl.pallas_call(
        functools.partial(_da_kernel, lam_init=lam_init, n_seg=len(key_slabs)),
        grid=(batch, qb),
        in_specs=in_specs,
        out_specs=pl.BlockSpec((tq, BR), lambda b, i: (b * qb + i, 0)),
        out_shape=jax.ShapeDtypeStruct((batch * nq, BR), BF16),
        compiler_params=_cparams(2),
        name="diff_attention",
    )(*args)


def _na_bias_tables(rpb):
    rows = 2048 // GRID_W
    geoms = [(0, 0), (NA_QROWS, 0), (rows - NA_QROWS, rows - NA_KROWS)]
    qr = np.arange(NA_QROWS)[:, None, None, None]
    qc = np.arange(GRID_W)[None, :, None, None]
    kr = np.arange(NA_KROWS)[None, None, :, None]
    kc = np.arange(GRID_W)[None, None, None, :]
    c0 = np.clip(qc - NA_WIN_W // 2, 0, GRID_W - NA_WIN_W)
    col_ok = (kc >= c0) & (kc < c0 + NA_WIN_W)
    dc = np.clip(kc - qc, 1 - NA_WIN_W, NA_WIN_W - 1) + NA_WIN_W - 1
    tables = []
    for q0, ws in geoms:
        r = q0 + qr
        r0 = np.clip(r - NA_WIN_H // 2, 0, rows - NA_WIN_H)
        kabs = ws + kr
        ok = np.broadcast_to((kabs >= r0) & (kabs < r0 + NA_WIN_H) & col_ok, (NA_QROWS, GRID_W, NA_KROWS, GRID_W))
        dr = np.clip(kabs - r + NA_WIN_H - 1, 0, 2 * NA_WIN_H - 2)
        dr_b, dc_b = np.broadcast_arrays(dr, dc)
        bias = rpb.astype(F32)[:, dr_b, dc_b]
        bias = jnp.where(ok[None], bias, NEG_INF)
        tables.append(bias.reshape(NA_HEADS, NA_QROWS * GRID_W, NA_KROWS * GRID_W))
    return jnp.stack(tables)


def _na_kernel(bias_ref, q_ref, kl_ref, vl_ref, kc_ref, vc_ref, o_ref):
    i = pl.program_id(1)
    n_steps = pl.num_programs(1)
    last_ws = 2048 // GRID_W - NA_KROWS
    ws = pl.multiple_of(jnp.clip(NA_QROWS * i - NA_WIN_H // 2, 0, last_ws) * GRID_W, NA_QROWS * GRID_W)
    geom = jnp.where(i == 0, 0, jnp.where(i == n_steps - 1, 2, 1))
    nkw = NA_KROWS * GRID_W
    outs = []
    for h in range(NA_HEADS):
        hs = slice(h * NA_DH, (h + 1) * NA_DH)
        q = q_ref[:, hs]
        s_w = _dot_nt(q, kl_ref[pl.ds(ws, nkw), hs]) + bias_ref[geom, h]
        s_c = _dot_nt(q, kc_ref[:, hs])
        m = jnp.maximum(s_w.max(axis=-1, keepdims=True), s_c.max(axis=-1, keepdims=True))
        e_w = jnp.exp(s_w - m)
        e_c = jnp.exp(s_c - m)
        den = e_w.sum(axis=-1, keepdims=True) + e_c.sum(axis=-1, keepdims=True)
        num = _dot(e_w.astype(BF16), vl_ref[pl.ds(ws, nkw), hs]) + _dot(e_c.astype(BF16), vc_ref[:, hs])
        outs.append(num / den)
    o_ref[...] = jnp.concatenate(outs, axis=-1).astype(BF16)


def _neigh_attention(bias, slab, slab_c, *, n, n_ctx):
    batch = slab.shape[0] // n
    tq = NA_QROWS * GRID_W
    steps = n // tq
    return pl.pallas_call(
        _na_kernel,
        grid=(batch, steps),
        in_specs=[
            _resident(bias.shape),
            pl.BlockSpec((tq, BR), lambda b, i: (b * steps + i, COL_QB)),
            pl.BlockSpec((n, BR), lambda b, i: (b, COL_KB)),
            pl.BlockSpec((n, BR), lambda b, i: (b, COL_VB)),
            pl.BlockSpec((n_ctx, BR), lambda b, i: (b, COL_KB)),
            pl.BlockSpec((n_ctx, BR), lambda b, i: (b, COL_VB)),
        ],
        out_specs=pl.BlockSpec((tq, BR), lambda b, i: (b * steps + i, 0)),
        out_shape=jax.ShapeDtypeStruct((batch * n, BR), BF16),
        compiler_params=_cparams(2),
        name="neigh_attention",
    )(bias, slab, slab, slab, slab_c, slab_c)


def _ctx_attn_kernel(q_ref, k_ref, v_ref, o_ref):
    outs = []
    for h in range(NA_HEADS):
        hs = slice(h * NA_DH, (h + 1) * NA_DH)
        outs.append(_softmax_pv(q_ref[:, hs], [(k_ref[:, hs], v_ref[:, hs])]))
    o_ref[...] = jnp.concatenate(outs, axis=-1).astype(BF16)


def _ctx_attention(slab_c, *, n_ctx):
    batch = slab_c.shape[0] // n_ctx
    return pl.pallas_call(
        _ctx_attn_kernel,
        grid=(batch,),
        in_specs=[pl.BlockSpec((n_ctx, BR), lambda b, c=c: (b, c)) for c in (COL_QB, COL_KB, COL_VB)],
        out_specs=pl.BlockSpec((n_ctx, BR), lambda b: (b, 0)),
        out_shape=jax.ShapeDtypeStruct((batch * n_ctx, BR), BF16),
        compiler_params=_cparams(1),
        name="ctx_attention",
    )(slab_c, slab_c, slab_c)


def _row_block_broadcast(b, size, offset):
    c = b.shape[0]
    if size >= 8:
        parts = [jnp.broadcast_to(b[s + offset:s + offset + 1, :], (size, b.shape[1])) for s in range(0, c, size)]
        return parts[0] if len(parts) == 1 else jnp.concatenate(parts, axis=0)
    pos = lax.broadcasted_iota(jnp.int32, b.shape, 0) % size
    out = b
    for p in range(size):
        if p != offset:
            out = jnp.where(pos == p, pltpu.roll(b, (p - offset) % c, 0), out)
    return out


def _hgrn_chunk(q, v, g, st_ref, d, gain_unused=None):
    c = HG_CHUNK
    rev = d == 1
    t_sq = lax.broadcasted_iota(jnp.int32, (c, c), 0)
    u_sq = lax.broadcasted_iota(jnp.int32, (c, c), 1)
    cum = jnp.where((u_sq >= t_sq) if rev else (u_sq <= t_sq), 1.0, 0.0).astype(BF16)
    g3 = _split3(g)
    b = _dot(cum, g3[0]) + _dot(cum, g3[1]) + _dot(cum, g3[2])
    btot = b[0:1, :] if rev else b[c - 1:c, :]
    k = 1.0 - jnp.exp(g)

    row = lax.broadcasted_iota(jnp.int32, (c, BR), 0)
    t4 = lax.broadcasted_iota(jnp.int32, (c, 4 * c), 0)
    s4 = lax.broadcasted_iota(jnp.int32, (c, 4 * c), 1) % c
    rb = lax.broadcasted_iota(jnp.int32, (4 * c, BR), 0) // c
    cb = lax.broadcasted_iota(jnp.int32, (4 * c, BR), 1) // HG_D
    head_diag = rb == cb

    def heads_block_diag(x):
        return jnp.where(head_diag, jnp.concatenate([x] * HG_HEADS, axis=0), 0.0).astype(BF16)

    att = jnp.where(t4 == s4, _dot_nt((q).astype(BF16), heads_block_diag(k)), 0.0)
    m = c // 2
    while m >= 1:
        ref_b = _row_block_broadcast(b, 2 * m, m if rev else m - 1)
        pos = row % (2 * m)
        q_side = (pos < m) if rev else (pos >= m)
        qm = jnp.where(q_side, q * jnp.exp(jnp.minimum(b - ref_b, 0.0)), 0.0)
        km = jnp.where(q_side, 0.0, k * jnp.exp(jnp.minimum(ref_b - b, 0.0)))
        same_block = (t4 // (2 * m)) == (s4 // (2 * m))
        att = att + jnp.where(same_block, _dot_nt(qm.astype(BF16), heads_block_diag(km)), 0.0)
        m //= 2

    st = st_ref[d]
    o = _dot(att.astype(BF16), heads_block_diag(v)) + _dot_nt((q * jnp.exp(b)).astype(BF16), st.astype(BF16))
    k_st = k * jnp.exp(btot - b)
    r2 = lax.broadcasted_iota(jnp.int32, (BR, BR), 0) // HG_D
    c2 = lax.broadcasted_iota(jnp.int32, (BR, BR), 1) // HG_D
    upd = jnp.where(r2 == c2, _dot_tn(v.astype(BF16), k_st.astype(BF16)), 0.0)
    st_ref[d] = st * jnp.exp(btot) + upd
    return o


def _hgrn_kernel(gain_ref, ql_ref, il_ref, gl_ref, qc_ref, ic_ref, gc_ref, yl_ref, yc_ref, st_ref, ol_ref, oc_ref):
    c = HG_CHUNK
    st_ref[...] = jnp.zeros(st_ref.shape, F32)

    def scan(q_ref, i_ref, g_ref, o_ref):
        n_chunks = q_ref.shape[0] // c

        def body(step, carry):
            for d in range(2):
                idx = step if d == 0 else n_chunks - 1 - step
                rows = pl.ds(pl.multiple_of(idx * c, c), c)
                o = _hgrn_chunk(q_ref[rows, :].astype(F32), i_ref[rows, :].astype(F32),
                                g_ref[rows, d * BR:(d + 1) * BR], st_ref, d)
                o_ref[d, rows, :] = o
            return carry

        lax.fori_loop(0, n_chunks, body, 0)

    scan(qc_ref, ic_ref, gc_ref, oc_ref)
    scan(ql_ref, il_ref, gl_ref, ol_ref)
    yl_ref[...] = _group_rmsnorm(ol_ref[0] + ol_ref[1], gain_ref[...], HG_D).astype(BF16)
    yc_ref[...] = _group_rmsnorm(oc_ref[0] + oc_ref[1], gain_ref[...], HG_D).astype(BF16)


def _hgrn(gain, slab, g, slab_c, g_c, *, n, n_ctx):
    batch = slab.shape[0] // n
    return pl.pallas_call(
        _hgrn_kernel,
        grid=(batch,),
        in_specs=[
            _resident((1, BR)),
            pl.BlockSpec((n, BR), lambda b: (b, COL_HQ)),
            pl.BlockSpec((n, BR), lambda b: (b, COL_HI)),
            pl.BlockSpec((n, 2 * BR), lambda b: (b, 0)),
            pl.BlockSpec((n_ctx, BR), lambda b: (b, COL_HQ)),
            pl.BlockSpec((n_ctx, BR), lambda b: (b, COL_HI)),
            pl.BlockSpec((n_ctx, 2 * BR), lambda b: (b, 0)),
        ],
        out_specs=[
            pl.BlockSpec((n, BR), lambda b: (b, 0)),
            pl.BlockSpec((n_ctx, BR), lambda b: (b, 0)),
        ],
        out_shape=[
            jax.ShapeDtypeStruct((batch * n, BR), BF16),
            jax.ShapeDtypeStruct((batch * n_ctx, BR), BF16),
        ],
        scratch_shapes=[
            pltpu.VMEM((2, BR, BR), F32),
            pltpu.VMEM((2, n, BR), F32),
            pltpu.VMEM((2, n_ctx, BR), F32),
        ],
        compiler_params=_cparams(1),
        name="hgrn2",
    )(gain, slab, slab, g, slab_c, slab_c, g_c)


def _dft_tables(n):
    lo = 64
    hi = n // lo
    kk = np.arange(n, dtype=np.int64)
    ang_hi = 2.0 * np.pi * ((np.arange(hi, dtype=np.int64)[:, None] * lo * kk[None, :]) % n) / n
    ang_lo = 2.0 * np.pi * ((np.arange(lo, dtype=np.int64)[:, None] * kk[None, :]) % n) / n
    ch, sh = (jnp.asarray(f(ang_hi), F32)[:, None, :] for f in (np.cos, np.sin))
    cl, sl = (jnp.asarray(f(ang_lo), F32)[None, :, :] for f in (np.cos, np.sin))
    scale = 1.0 / math.sqrt(n)
    cos = ((ch * cl - sh * sl) * scale).reshape(n, n)
    sin = ((sh * cl + ch * sl) * scale).reshape(n, n)
    return jnp.concatenate([cos, -sin], axis=1).astype(BF16)


def _channel_dft_tables():
    j = np.arange(BR)
    same = (j[:, None] // FT_DG) == (j[None, :] // FT_DG)
    ang = 2.0 * np.pi * (((j[:, None] % FT_DG) * (j[None, :] % FT_DG)) % FT_DG) / FT_DG
    scale = 1.0 / math.sqrt(FT_DG)
    return (jnp.asarray(np.where(same, np.cos(ang), 0.0) * scale, BF16),
            jnp.asarray(np.where(same, np.sin(ang), 0.0) * scale, BF16))


def _fourier_kernel(cs_ref, bc_ref, bs_ref, u_ref, o_ref):
    u = u_ref[...]
    stacked = jnp.concatenate([_dot(u, bc_ref[...]).astype(BF16), _dot(u, bs_ref[...]).astype(BF16)], axis=0)
    o_ref[...] = _dot(cs_ref[...], stacked).astype(BF16)


def _fourier(cs, bc, bs, slab, *, n):
    batch = slab.shape[0] // n
    return pl.pallas_call(
        _fourier_kernel,
        grid=(batch,),
        in_specs=[
            _resident((n, 2 * n)),
            _resident((BR, BR)),
            _resident((BR, BR)),
            pl.BlockSpec((n, BR), lambda b: (b, COL_U)),
        ],
        out_specs=pl.BlockSpec((n, BR), lambda b: (b, 0)),
        out_shape=jax.ShapeDtypeStruct((batch * n, BR), BF16),
        compiler_params=_cparams(1),
        name="fourier",
    )(cs, bc, bs, slab)


def _merge_kernel(x_ref, mod_ref, ng_ref, ya_ref, yb_ref, yh_ref, yf_ref, sg_ref, wup_ref, wmg_ref, wout_ref, o_ref):
    x = x_ref[...]
    h = _modulated_norm(x, mod_ref, ng_ref).astype(BF16)
    acc = None
    for i, y_ref in enumerate((ya_ref, yb_ref, yh_ref, yf_ref)):
        y = (y_ref[...].astype(F32) * sg_ref[:, i * BR:(i + 1) * BR].astype(F32)).astype(BF16)
        term = jax.nn.sigmoid(_dot(h, wmg_ref[i])) * _dot(y, wup_ref[i])
        acc = term if acc is None else acc + term
    o_ref[...] = x + mod_ref[2:3, :] * _dot(acc.astype(BF16), wout_ref[...])


def _merge(x2d, mod_l, ng, ys, slab, wup, wmg, wout, *, tm, rows_per_mod, mod_row0):
    n = x2d.shape[0]

    def mod_map(t):
        return ((t * tm) // rows_per_mod + mod_row0, 0, 0)

    return pl.pallas_call(
        _merge_kernel,
        grid=(n // tm,),
        in_specs=[
            pl.BlockSpec((tm, D_MODEL), lambda t: (t, 0)),
            pl.BlockSpec((None, 3, D_MODEL), mod_map),
            _resident((1, D_MODEL)),
            *[pl.BlockSpec((tm, BR), lambda t: (t, 0)) for _ in range(4)],
            pl.BlockSpec((tm, 4 * BR), lambda t: (t, COL_SG)),
            _resident((4, BR, D_MODEL)),
            _resident((4, D_MODEL, D_MODEL)),
            _resident((D_MODEL, D_MODEL)),
        ],
        out_specs=pl.BlockSpec((tm, D_MODEL), lambda t: (t, 0)),
        out_shape=jax.ShapeDtypeStruct((n, D_MODEL), F32),
        compiler_params=_cparams(1),
        name="merge",
    )(x2d, mod_l, ng, *ys, slab, wup, wmg, wout)


def _rope_tables(n):
    t = jnp.arange(n)
    row = (t // GRID_W).astype(F32)
    col = (t % GRID_W).astype(F32)
    d_axis = DA_DH // 2
    inv = ROPE_THETA ** (-jnp.arange(0, d_axis, 2, dtype=F32) / d_axis)
    ang = jnp.concatenate([row[:, None] * inv, col[:, None] * inv], axis=-1)
    cos, sin = jnp.cos(ang), jnp.sin(ang)
    cos_h = jnp.concatenate([cos, cos], axis=-1)
    sin_h = jnp.concatenate([-sin, sin], axis=-1)
    return jnp.tile(cos_h, (1, 2 * DA_HEADS)), jnp.tile(sin_h, (1, 2 * DA_HEADS))


def kernel(x, c, ctx, c_ctx, norm_gain, w_mod, b_mod, w_in, da_qk_gain, da_lambda, da_subln_gain,
           na_qk_gain, na_rpb, hg_lb_logits, hg_norm_gain, w_up, w_merge, w_out):
    batch, n, _ = x.shape
    n_ctx = ctx.shape[1]
    assert n == 2048 and n % GRID_W == 0 and batch + 1 <= MOD_ROWS

    cc = jnp.concatenate([c, c_ctx[None, :], jnp.zeros((MOD_ROWS - batch - 1, D_MODEL), F32)], axis=0)
    mod = _modulation(cc, w_mod, b_mod).reshape(DEPTH, MOD_ROWS, 3, D_MODEL)

    p_lb = jax.nn.softmax(hg_lb_logits.astype(F32), axis=1)
    lower = jnp.cumsum(p_lb, axis=1) - p_lb[:, :1]
    log_lb = jnp.log(jnp.maximum(lower, LB_FLOOR))
    log_1mlb = jnp.log1p(-lower)
    lv = da_lambda.astype(F32)
    lam_all = jnp.exp(jnp.sum(lv[:, 0] * lv[:, 1], axis=-1)) - jnp.exp(jnp.sum(lv[:, 2] * lv[:, 3], axis=-1))

    cos, sin = _rope_tables(n)
    cs_lat, cs_ctx = _dft_tables(n), _dft_tables(n_ctx)
    bc, bs = _channel_dft_tables()

    xl = x.reshape(batch * n, D_MODEL)
    xc = ctx.reshape(batch * n_ctx, D_MODEL)
    for l in range(DEPTH):
        need_ctx = l < DEPTH - 1
        lam_init = 0.8 - 0.6 * math.exp(-0.3 * l)
        lam = (lam_all[l] + lam_init).reshape(1)
        ng = norm_gain[l].reshape(1, D_MODEL)
        w = w_in[l].astype(BF16)
        gda = jnp.tile(da_qk_gain[l], (1, BR // DA_DH))
        gna = jnp.tile(na_qk_gain[l], (1, BR // NA_DH))
        lbp = jnp.stack([log_lb[0, l], log_1mlb[0, l], log_lb[1, l], log_1mlb[1, l]])
        subln = da_subln_gain[l].reshape(1, DA_DV)
        hgain = jnp.tile(hg_norm_gain[l].reshape(1, HG_D), (1, HG_HEADS))
        bias = _na_bias_tables(na_rpb[l])

        slab, g = _inproj(xl, mod[l], ng, w, gda, gna, lbp, cos, sin,
                          tm=512, rows_per_mod=n, mod_row0=0, rope=True)
        slab_c, g_c = _inproj(xc, mod[l], ng, w, gda, gna, lbp, cos[:n_ctx], sin[:n_ctx],
                              tm=n_ctx, rows_per_mod=batch * n_ctx, mod_row0=batch, rope=False)

        ya = _diff_attention(lam, subln, slab, [(slab, n), (slab_c, n_ctx)], nq=n, tq=512, lam_init=lam_init)
        yb = _neigh_attention(bias, slab, slab_c, n=n, n_ctx=n_ctx)
        yh, yh_c = _hgrn(hgain, slab, g, slab_c, g_c, n=n, n_ctx=n_ctx)
        yf = _fourier(cs_lat, bc, bs, slab, n=n)

        wup = w_up[l].astype(BF16)
        wmg = w_merge[l].astype(BF16)
        wout = w_out[l].astype(BF16)
        xl = _merge(xl, mod[l], ng, (ya, yb, yh, yf), slab, wup, wmg, wout,
                    tm=512, rows_per_mod=n, mod_row0=0)
        if need_ctx:
            ya_c = _diff_attention(lam, subln, slab_c, [(slab_c, n_ctx)], nq=n_ctx, tq=n_ctx, lam_init=lam_init)
            yb_c = _ctx_attention(slab_c, n_ctx=n_ctx)
            yf_c = _fourier(cs_ctx, bc, bs, slab_c, n=n_ctx)
            xc = _merge(xc, mod[l], ng, (ya_c, yb_c, yh_c, yf_c), slab_c, wup, wmg, wout,
                        tm=n_ctx, rows_per_mod=batch * n_ctx, mod_row0=batch)
    return xl.reshape(batch, n, D_MODEL)
```

```python
import functools
import math

import numpy as np
import jax
import jax.numpy as jnp
from jax import lax
from jax.experimental import pallas as pl
from jax.experimental.pallas import tpu as pltpu

D_MODEL = 1024
DEPTH = 4
GRID_W = 64
BR = 256
DA_HEADS, DA_DH, DA_DV = 4, 32, 64
NA_HEADS, NA_DH, NA_WIN_H, NA_WIN_W = 4, 64, 8, 16
HG_HEADS, HG_D, HG_CHUNK = 4, 64, 64
FT_DG = 64
IN_WIDTH = 15 * BR
LB_FLOOR = 1e-20
ROPE_THETA = 10000.0
EPS = 1e-6
NEG_INF = -1e30

COL_SG = 0
COL_QA, COL_KA, COL_VA, COL_QB, COL_KB, COL_VB, COL_HQ, COL_HI, COL_U = 4, 5, 6, 7, 8, 9, 10, 11, 12
SLAB_W = 13 * BR

NA_QROWS = 4
NA_KROWS = 12
MOD_ROWS = 24

VMEM_LIMIT_BYTES = 56 * 1024 * 1024

F32 = jnp.float32
BF16 = jnp.bfloat16


def _cparams(n_axes):
    return pltpu.CompilerParams(dimension_semantics=("arbitrary",) * n_axes, vmem_limit_bytes=VMEM_LIMIT_BYTES)


def _resident(shape):
    nd = len(shape)
    return pl.BlockSpec(shape, lambda *_: (0,) * nd, pipeline_mode=pl.Buffered(1))


def _dot(a, b):
    return jnp.dot(a, b, preferred_element_type=F32)


def _dot_nt(a, b):
    return lax.dot_general(a, b, (((1,), (1,)), ((), ())), preferred_element_type=F32)


def _dot_tn(a, b):
    return lax.dot_general(a, b, (((0,), (0,)), ((), ())), preferred_element_type=F32)


def _split3(v):
    hi = v.astype(BF16)
    r = v - hi.astype(F32)
    mid = r.astype(BF16)
    lo = (r - mid.astype(F32)).astype(BF16)
    return hi, mid, lo


def _group_mean(v, gsize):
    w = v.shape[-1]
    r = lax.broadcasted_iota(jnp.int32, (w, w), 0) // gsize
    c = lax.broadcasted_iota(jnp.int32, (w, w), 1) // gsize
    ones = jnp.where(r == c, 1.0 / gsize, 0.0).astype(BF16)
    hi, mid, lo = _split3(v)
    return _dot(hi, ones) + _dot(mid, ones) + _dot(lo, ones)


def _group_rmsnorm(v, gain, gsize):
    return v * lax.rsqrt(_group_mean(v * v, gsize) + EPS) * gain


def _silu(v):
    return v * jax.nn.sigmoid(v)


def _mod_kernel(c_ref, w_ref, b_ref, o_ref):
    s = _silu(c_ref[...]).astype(BF16)
    o_ref[...] = _dot(s, w_ref[...].astype(BF16)) + b_ref[...]


def _modulation(cc, w_mod, b_mod):
    tn = D_MODEL
    return pl.pallas_call(
        _mod_kernel,
        grid=(DEPTH, 3 * D_MODEL // tn),
        in_specs=[
            pl.BlockSpec((MOD_ROWS, D_MODEL), lambda l, j: (0, 0)),
            pl.BlockSpec((None, D_MODEL, tn), lambda l, j: (l, 0, j)),
            pl.BlockSpec((None, 1, tn), lambda l, j: (l, 0, j)),
        ],
        out_specs=pl.BlockSpec((None, MOD_ROWS, tn), lambda l, j: (l, 0, j)),
        out_shape=jax.ShapeDtypeStruct((DEPTH, MOD_ROWS, 3 * D_MODEL), F32),
        compiler_params=_cparams(2),
        name="modulation",
    )(cc, w_mod, b_mod.reshape(DEPTH, 1, 3 * D_MODEL))


def _modulated_norm(x, mod_ref, ng_ref):
    ms = jnp.mean(x * x, axis=-1, keepdims=True)
    y = x * lax.rsqrt(ms + EPS) * ng_ref[...]
    return y * (1.0 + mod_ref[1:2, :]) + mod_ref[0:1, :]


def _log_forget(f, log_lb, log_1mlb):
    ls = jnp.minimum(f, 0.0) - jnp.log1p(jnp.exp(-jnp.abs(f)))
    a = log_lb
    b = log_1mlb + ls
    return jnp.maximum(a, b) + jnp.log1p(jnp.exp(-jnp.abs(a - b)))


def _rope(v, cos, sin_signed):
    lane = lax.broadcasted_iota(jnp.int32, v.shape, 1)
    first_half = (lane % DA_DH) < (DA_DH // 2)
    swapped = jnp.where(first_half, pltpu.roll(v, BR - DA_DH // 2, 1), pltpu.roll(v, DA_DH // 2, 1))
    return v * cos + swapped * sin_signed


def _inproj_kernel(x_ref, mod_ref, ng_ref, w_ref, gda_ref, gna_ref, lb_ref, cos_ref, sin_ref, slab_ref, g_ref, *, rope):
    h = _modulated_norm(x_ref[...], mod_ref, ng_ref).astype(BF16)

    def proj(col):
        return _dot(h, w_ref[:, col * BR:(col + 1) * BR])

    def put(col, v):
        slab_ref[:, col * BR:(col + 1) * BR] = v.astype(BF16)

    q = _group_rmsnorm(proj(0), gda_ref[0:1, :], DA_DH)
    k = _group_rmsnorm(proj(1), gda_ref[1:2, :], DA_DH)
    if rope:
        q = _rope(q, cos_ref[...], sin_ref[...])
        k = _rope(k, cos_ref[...], sin_ref[...])
    put(COL_QA, q * (DA_DH ** -0.5))
    put(COL_KA, k)
    put(COL_VA, proj(2))
    put(COL_QB, _group_rmsnorm(proj(3), gna_ref[0:1, :], NA_DH) * (NA_DH ** -0.5))
    put(COL_KB, _group_rmsnorm(proj(4), gna_ref[1:2, :], NA_DH))
    put(COL_VB, proj(5))
    put(COL_HQ, _silu(proj(6)))
    g_ref[:, 0:BR] = _log_forget(proj(7), lb_ref[0:1, :], lb_ref[1:2, :])
    g_ref[:, BR:2 * BR] = _log_forget(proj(8), lb_ref[2:3, :], lb_ref[3:4, :])
    put(COL_HI, proj(9))
    put(COL_U, proj(10))
    for i in range(4):
        put(COL_SG + i, _silu(proj(11 + i)))


def _inproj(x2d, mod_l, ng, w, gda, gna, lbp, cos, sin, *, tm, rows_per_mod, mod_row0, rope):
    n = x2d.shape[0]
    n_pos = cos.shape[0]
    tiles_per_seq = n_pos // tm

    def mod_map(t):
        return ((t * tm) // rows_per_mod + mod_row0, 0, 0)

    return pl.pallas_call(
        functools.partial(_inproj_kernel, rope=rope),
        grid=(n // tm,),
        in_specs=[
            pl.BlockSpec((tm, D_MODEL), lambda t: (t, 0)),
            pl.BlockSpec((None, 3, D_MODEL), mod_map),
            _resident((1, D_MODEL)),
            _resident((D_MODEL, IN_WIDTH)),
            _resident((2, BR)),
            _resident((2, BR)),
            _resident((4, BR)),
            pl.BlockSpec((tm, BR), lambda t: (t % tiles_per_seq, 0)),
            pl.BlockSpec((tm, BR), lambda t: (t % tiles_per_seq, 0)),
        ],
        out_specs=[
            pl.BlockSpec((tm, SLAB_W), lambda t: (t, 0)),
            pl.BlockSpec((tm, 2 * BR), lambda t: (t, 0)),
        ],
        out_shape=[
            jax.ShapeDtypeStruct((n, SLAB_W), BF16),
            jax.ShapeDtypeStruct((n, 2 * BR), F32),
        ],
        compiler_params=_cparams(1),
        name="inproj_rope" if rope else "inproj",
    )(x2d, mod_l, ng, w, gda, gna, lbp, cos, sin)


def _softmax_pv(q, segments):
    scores = [_dot_nt(q, k) for k, _ in segments]
    m = scores[0].max(axis=-1, keepdims=True)
    for s in scores[1:]:
        m = jnp.maximum(m, s.max(axis=-1, keepdims=True))
    num, den = None, None
    for s, (_, v) in zip(scores, segments):
        e = jnp.exp(s - m)
        d = e.sum(axis=-1, keepdims=True)
        p = _dot(e.astype(BF16), v)
        num = p if num is None else num + p
        den = d if den is None else den + d
    return num / den


def _da_kernel(lam_ref, g_ref, q_ref, *refs, lam_init, n_seg):
    kv_refs, o_ref = refs[:2 * n_seg], refs[2 * n_seg]
    lam = lam_ref[0]
    outs = []
    for h in range(DA_HEADS):
        vs = slice(h * DA_DV, (h + 1) * DA_DV)
        att = []
        for j in range(2):
            qs = slice((2 * h + j) * DA_DH, (2 * h + j + 1) * DA_DH)
            segs = [(kv_refs[2 * s][:, qs], kv_refs[2 * s + 1][:, vs]) for s in range(n_seg)]
            att.append(_softmax_pv(q_ref[:, qs], segs))
        o = att[0] - lam * att[1]
        o = o * lax.rsqrt(jnp.mean(o * o, axis=-1, keepdims=True) + EPS) * g_ref[...] * (1.0 - lam_init)
        outs.append(o)
    o_ref[...] = jnp.concatenate(outs, axis=-1).astype(BF16)


def _diff_attention(lam, subln, slab_q, key_slabs, *, nq, tq, lam_init):
    batch = slab_q.shape[0] // nq
    qb = nq // tq
    in_specs = [
        pl.BlockSpec(memory_space=pltpu.SMEM),
        _resident((1, DA_DV)),
        pl.BlockSpec((tq, BR), lambda b, i: (b * qb + i, COL_QA)),
    ]
    args = [lam, subln, slab_q]
    for slab, nk in key_slabs:
        in_specs += [pl.BlockSpec((nk, BR), lambda b, i: (b, COL_KA)), pl.BlockSpec((nk, BR), lambda b, i: (b, COL_VA))]
        args += [slab, slab]
    return pl.pallas_call(
        functools.partial(_da_kernel, lam_init=lam_init, n_seg=len(key_slabs)),
        grid=(batch, qb),
        in_specs=in_specs,
        out_specs=pl.BlockSpec((tq, BR), lambda b, i: (b * qb + i, 0)),
        out_shape=jax.ShapeDtypeStruct((batch * nq, BR), BF16),
        compiler_params=_cparams(2),
        name="diff_attention",
    )(*args)


def _na_bias_tables(rpb):
    rows = 2048 // GRID_W
    n_dr, n_dc = 2 * NA_WIN_H - 1, 2 * NA_WIN_W - 1
    geoms = [(0, 0), (NA_QROWS, 0), (rows - NA_QROWS, rows - NA_KROWS)]
    qc = np.arange(GRID_W)[:, None]
    kc = np.arange(GRID_W)[None, :]
    c0 = np.clip(qc - NA_WIN_W // 2, 0, GRID_W - NA_WIN_W)
    col_ok = (kc >= c0) & (kc < c0 + NA_WIN_W)
    dc = np.clip(kc - qc, 1 - NA_WIN_W, NA_WIN_W - 1) + NA_WIN_W - 1
    col_onehot = (dc[None] == np.arange(n_dc)[:, None, None]).astype(np.float32)
    qr = np.arange(NA_QROWS)[:, None]
    kr = np.arange(NA_KROWS)[None, :]
    row_onehot = np.zeros((3, NA_QROWS, NA_KROWS, n_dr), np.float32)
    ok = np.zeros((3, NA_QROWS, GRID_W, NA_KROWS, GRID_W), bool)
    for gi, (q0, ws) in enumerate(geoms):
        r = q0 + qr
        r0 = np.clip(r - NA_WIN_H // 2, 0, rows - NA_WIN_H)
        kabs = ws + kr
        row_ok = (kabs >= r0) & (kabs < r0 + NA_WIN_H)
        dr = kabs - r + NA_WIN_H - 1
        row_onehot[gi] = (dr[..., None] == np.arange(n_dr)) & row_ok[..., None]
        ok[gi] = row_ok[:, None, :, None] & col_ok[None, :, None, :]
    by_col = jnp.einsum("lhrd,dqc->lhrqc", rpb.astype(F32), col_onehot, precision=lax.Precision.HIGHEST)
    bias = jnp.einsum("gakr,lhrqc->lghaqkc", row_onehot, by_col, precision=lax.Precision.HIGHEST)
    bias = jnp.where(ok[None, :, None], bias, NEG_INF)
    return bias.reshape(rpb.shape[0], 3, NA_HEADS, NA_QROWS * GRID_W, NA_KROWS * GRID_W)


def _na_kernel(bias_ref, q_ref, kl_ref, vl_ref, kc_ref, vc_ref, o_ref):
    i = pl.program_id(1)
    n_steps = pl.num_programs(1)
    last_ws = 2048 // GRID_W - NA_KROWS
    ws = pl.multiple_of(jnp.clip(NA_QROWS * i - NA_WIN_H // 2, 0, last_ws) * GRID_W, NA_QROWS * GRID_W)
    geom = jnp.where(i == 0, 0, jnp.where(i == n_steps - 1, 2, 1))
    nkw = NA_KROWS * GRID_W
    outs = []
    for h in range(NA_HEADS):
        hs = slice(h * NA_DH, (h + 1) * NA_DH)
        q = q_ref[:, hs]
        s_w = _dot_nt(q, kl_ref[pl.ds(ws, nkw), hs]) + bias_ref[geom, h]
        s_c = _dot_nt(q, kc_ref[:, hs])
        m = jnp.maximum(s_w.max(axis=-1, keepdims=True), s_c.max(axis=-1, keepdims=True))
        e_w = jnp.exp(s_w - m)
        e_c = jnp.exp(s_c - m)
        den = e_w.sum(axis=-1, keepdims=True) + e_c.sum(axis=-1, keepdims=True)
        num = _dot(e_w.astype(BF16), vl_ref[pl.ds(ws, nkw), hs]) + _dot(e_c.astype(BF16), vc_ref[:, hs])
        outs.append(num / den)
    o_ref[...] = jnp.concatenate(outs, axis=-1).astype(BF16)


def _neigh_attention(bias, slab, slab_c, *, n, n_ctx):
    batch = slab.shape[0] // n
    tq = NA_QROWS * GRID_W
    steps = n // tq
    return pl.pallas_call(
        _na_kernel,
        grid=(batch, steps),
        in_specs=[
            _resident(bias.shape),
            pl.BlockSpec((tq, BR), lambda b, i: (b * steps + i, COL_QB)),
            pl.BlockSpec((n, BR), lambda b, i: (b, COL_KB)),
            pl.BlockSpec((n, BR), lambda b, i: (b, COL_VB)),
            pl.BlockSpec((n_ctx, BR), lambda b, i: (b, COL_KB)),
            pl.BlockSpec((n_ctx, BR), lambda b, i: (b, COL_VB)),
        ],
        out_specs=pl.BlockSpec((tq, BR), lambda b, i: (b * steps + i, 0)),
        out_shape=jax.ShapeDtypeStruct((batch * n, BR), BF16),
        compiler_params=_cparams(2),
        name="neigh_attention",
    )(bias, slab, slab, slab, slab_c, slab_c)


def _ctx_attn_kernel(q_ref, k_ref, v_ref, o_ref):
    outs = []
    for h in range(NA_HEADS):
        hs = slice(h * NA_DH, (h + 1) * NA_DH)
        outs.append(_softmax_pv(q_ref[:, hs], [(k_ref[:, hs], v_ref[:, hs])]))
    o_ref[...] = jnp.concatenate(outs, axis=-1).astype(BF16)


def _ctx_attention(slab_c, *, n_ctx):
    batch = slab_c.shape[0] // n_ctx
    return pl.pallas_call(
        _ctx_attn_kernel,
        grid=(batch,),
        in_specs=[pl.BlockSpec((n_ctx, BR), lambda b, c=c: (b, c)) for c in (COL_QB, COL_KB, COL_VB)],
        out_specs=pl.BlockSpec((n_ctx, BR), lambda b: (b, 0)),
        out_shape=jax.ShapeDtypeStruct((batch * n_ctx, BR), BF16),
        compiler_params=_cparams(1),
        name="ctx_attention",
    )(slab_c, slab_c, slab_c)


def _row_block_broadcast(b, size, offset):
    c = b.shape[0]
    if size >= 8:
        parts = [jnp.broadcast_to(b[s + offset:s + offset + 1, :], (size, b.shape[1])) for s in range(0, c, size)]
        return parts[0] if len(parts) == 1 else jnp.concatenate(parts, axis=0)
    pos = lax.broadcasted_iota(jnp.int32, b.shape, 0) % size
    out = b
    for p in range(size):
        if p != offset:
            out = jnp.where(pos == p, pltpu.roll(b, (p - offset) % c, 0), out)
    return out


def _hgrn_chunk(q, v, g, st_ref, d, gain_unused=None):
    c = HG_CHUNK
    rev = d == 1
    t_sq = lax.broadcasted_iota(jnp.int32, (c, c), 0)
    u_sq = lax.broadcasted_iota(jnp.int32, (c, c), 1)
    cum = jnp.where((u_sq >= t_sq) if rev else (u_sq <= t_sq), 1.0, 0.0).astype(BF16)
    g3 = _split3(g)
    b = _dot(cum, g3[0]) + _dot(cum, g3[1]) + _dot(cum, g3[2])
    btot = b[0:1, :] if rev else b[c - 1:c, :]
    k = 1.0 - jnp.exp(g)

    row = lax.broadcasted_iota(jnp.int32, (c, BR), 0)
    t4 = lax.broadcasted_iota(jnp.int32, (c, 4 * c), 0)
    s4 = lax.broadcasted_iota(jnp.int32, (c, 4 * c), 1) % c
    rb = lax.broadcasted_iota(jnp.int32, (4 * c, BR), 0) // c
    cb = lax.broadcasted_iota(jnp.int32, (4 * c, BR), 1) // HG_D
    head_diag = rb == cb

    def heads_block_diag(x):
        return jnp.where(head_diag, jnp.concatenate([x] * HG_HEADS, axis=0), 0.0).astype(BF16)

    att = jnp.where(t4 == s4, _dot_nt((q).astype(BF16), heads_block_diag(k)), 0.0)
    m = c // 2
    while m >= 1:
        ref_b = _row_block_broadcast(b, 2 * m, m if rev else m - 1)
        pos = row % (2 * m)
        q_side = (pos < m) if rev else (pos >= m)
        qm = jnp.where(q_side, q * jnp.exp(jnp.minimum(b - ref_b, 0.0)), 0.0)
        km = jnp.where(q_side, 0.0, k * jnp.exp(jnp.minimum(ref_b - b, 0.0)))
        same_block = (t4 // (2 * m)) == (s4 // (2 * m))
        att = att + jnp.where(same_block, _dot_nt(qm.astype(BF16), heads_block_diag(km)), 0.0)
        m //= 2

    st = st_ref[d]
    o = _dot(att.astype(BF16), heads_block_diag(v)) + _dot_nt((q * jnp.exp(b)).astype(BF16), st.astype(BF16))
    k_st = k * jnp.exp(btot - b)
    r2 = lax.broadcasted_iota(jnp.int32, (BR, BR), 0) // HG_D
    c2 = lax.broadcasted_iota(jnp.int32, (BR, BR), 1) // HG_D
    upd = jnp.where(r2 == c2, _dot_tn(v.astype(BF16), k_st.astype(BF16)), 0.0)
    st_ref[d] = st * jnp.exp(btot) + upd
    return o


def _hgrn_kernel(gain_ref, ql_ref, il_ref, gl_ref, qc_ref, ic_ref, gc_ref, yl_ref, yc_ref, st_ref, ol_ref, oc_ref):
    c = HG_CHUNK
    st_ref[...] = jnp.zeros(st_ref.shape, F32)

    def scan(q_ref, i_ref, g_ref, o_ref):
        n_chunks = q_ref.shape[0] // c

        def body(step, carry):
            for d in range(2):
                idx = step if d == 0 else n_chunks - 1 - step
                rows = pl.ds(pl.multiple_of(idx * c, c), c)
                o = _hgrn_chunk(q_ref[rows, :].astype(F32), i_ref[rows, :].astype(F32),
                                g_ref[rows, d * BR:(d + 1) * BR], st_ref, d)
                o_ref[d, rows, :] = o
            return carry

        lax.fori_loop(0, n_chunks, body, 0)

    scan(qc_ref, ic_ref, gc_ref, oc_ref)
    scan(ql_ref, il_ref, gl_ref, ol_ref)
    yl_ref[...] = _group_rmsnorm(ol_ref[0] + ol_ref[1], gain_ref[...], HG_D).astype(BF16)
    yc_ref[...] = _group_rmsnorm(oc_ref[0] + oc_ref[1], gain_ref[...], HG_D).astype(BF16)


def _hgrn(gain, slab, g, slab_c, g_c, *, n, n_ctx):
    batch = slab.shape[0] // n
    return pl.pallas_call(
        _hgrn_kernel,
        grid=(batch,),
        in_specs=[
            _resident((1, BR)),
            pl.BlockSpec((n, BR), lambda b: (b, COL_HQ)),
            pl.BlockSpec((n, BR), lambda b: (b, COL_HI)),
            pl.BlockSpec((n, 2 * BR), lambda b: (b, 0)),
            pl.BlockSpec((n_ctx, BR), lambda b: (b, COL_HQ)),
            pl.BlockSpec((n_ctx, BR), lambda b: (b, COL_HI)),
            pl.BlockSpec((n_ctx, 2 * BR), lambda b: (b, 0)),
        ],
        out_specs=[
            pl.BlockSpec((n, BR), lambda b: (b, 0)),
            pl.BlockSpec((n_ctx, BR), lambda b: (b, 0)),
        ],
        out_shape=[
            jax.ShapeDtypeStruct((batch * n, BR), BF16),
            jax.ShapeDtypeStruct((batch * n_ctx, BR), BF16),
        ],
        scratch_shapes=[
            pltpu.VMEM((2, BR, BR), F32),
            pltpu.VMEM((2, n, BR), F32),
            pltpu.VMEM((2, n_ctx, BR), F32),
        ],
        compiler_params=_cparams(1),
        name="hgrn2",
    )(gain, slab, slab, g, slab_c, slab_c, g_c)


def _dft_tables(n):
    lo = 64
    hi = n // lo
    kk = np.arange(n, dtype=np.int64)
    ang_hi = 2.0 * np.pi * ((np.arange(hi, dtype=np.int64)[:, None] * lo * kk[None, :]) % n) / n
    ang_lo = 2.0 * np.pi * ((np.arange(lo, dtype=np.int64)[:, None] * kk[None, :]) % n) / n
    ch, sh = (jnp.asarray(f(ang_hi), F32)[:, None, :] for f in (np.cos, np.sin))
    cl, sl = (jnp.asarray(f(ang_lo), F32)[None, :, :] for f in (np.cos, np.sin))
    scale = 1.0 / math.sqrt(n)
    cos = ((ch * cl - sh * sl) * scale).reshape(n, n)
    sin = ((sh * cl + ch * sl) * scale).reshape(n, n)
    return jnp.concatenate([cos, -sin], axis=1).astype(BF16)


def _channel_dft_tables():
    j = np.arange(BR)
    same = (j[:, None] // FT_DG) == (j[None, :] // FT_DG)
    ang = 2.0 * np.pi * (((j[:, None] % FT_DG) * (j[None, :] % FT_DG)) % FT_DG) / FT_DG
    scale = 1.0 / math.sqrt(FT_DG)
    return (jnp.asarray(np.where(same, np.cos(ang), 0.0) * scale, BF16),
            jnp.asarray(np.where(same, np.sin(ang), 0.0) * scale, BF16))


def _fourier_kernel(cs_ref, bc_ref, bs_ref, u_ref, o_ref):
    u = u_ref[...]
    stacked = jnp.concatenate([_dot(u, bc_ref[...]).astype(BF16), _dot(u, bs_ref[...]).astype(BF16)], axis=0)
    o_ref[...] = _dot(cs_ref[...], stacked).astype(BF16)


def _fourier(cs, bc, bs, slab, *, n):
    batch = slab.shape[0] // n
    return pl.pallas_call(
        _fourier_kernel,
        grid=(batch,),
        in_specs=[
            _resident((n, 2 * n)),
            _resident((BR, BR)),
            _resident((BR, BR)),
            pl.BlockSpec((n, BR), lambda b: (b, COL_U)),
        ],
        out_specs=pl.BlockSpec((n, BR), lambda b: (b, 0)),
        out_shape=jax.ShapeDtypeStruct((batch * n, BR), BF16),
        compiler_params=_cparams(1),
        name="fourier",
    )(cs, bc, bs, slab)


def _merge_kernel(x_ref, mod_ref, ng_ref, ya_ref, yb_ref, yh_ref, yf_ref, sg_ref, wup_ref, wmg_ref, wout_ref, o_ref):
    x = x_ref[...]
    h = _modulated_norm(x, mod_ref, ng_ref).astype(BF16)
    acc = None
    for i, y_ref in enumerate((ya_ref, yb_ref, yh_ref, yf_ref)):
        y = (y_ref[...].astype(F32) * sg_ref[:, i * BR:(i + 1) * BR].astype(F32)).astype(BF16)
        term = jax.nn.sigmoid(_dot(h, wmg_ref[i])) * _dot(y, wup_ref[i])
        acc = term if acc is None else acc + term
    o_ref[...] = x + mod_ref[2:3, :] * _dot(acc.astype(BF16), wout_ref[...])


def _merge(x2d, mod_l, ng, ys, slab, wup, wmg, wout, *, tm, rows_per_mod, mod_row0):
    n = x2d.shape[0]

    def mod_map(t):
        return ((t * tm) // rows_per_mod + mod_row0, 0, 0)

    return pl.pallas_call(
        _merge_kernel,
        grid=(n // tm,),
        in_specs=[
            pl.BlockSpec((tm, D_MODEL), lambda t: (t, 0)),
            pl.BlockSpec((None, 3, D_MODEL), mod_map),
            _resident((1, D_MODEL)),
            *[pl.BlockSpec((tm, BR), lambda t: (t, 0)) for _ in range(4)],
            pl.BlockSpec((tm, 4 * BR), lambda t: (t, COL_SG)),
            _resident((4, BR, D_MODEL)),
            _resident((4, D_MODEL, D_MODEL)),
            _resident((D_MODEL, D_MODEL)),
        ],
        out_specs=pl.BlockSpec((tm, D_MODEL), lambda t: (t, 0)),
        out_shape=jax.ShapeDtypeStruct((n, D_MODEL), F32),
        compiler_params=_cparams(1),
        name="merge",
    )(x2d, mod_l, ng, *ys, slab, wup, wmg, wout)


def _rope_tables(n):
    t = jnp.arange(n)
    row = (t // GRID_W).astype(F32)
    col = (t % GRID_W).astype(F32)
    d_axis = DA_DH // 2
    inv = ROPE_THETA ** (-jnp.arange(0, d_axis, 2, dtype=F32) / d_axis)
    ang = jnp.concatenate([row[:, None] * inv, col[:, None] * inv], axis=-1)
    cos, sin = jnp.cos(ang), jnp.sin(ang)
    cos_h = jnp.concatenate([cos, cos], axis=-1)
    sin_h = jnp.concatenate([-sin, sin], axis=-1)
    return jnp.tile(cos_h, (1, 2 * DA_HEADS)), jnp.tile(sin_h, (1, 2 * DA_HEADS))


def kernel(x, c, ctx, c_ctx, norm_gain, w_mod, b_mod, w_in, da_qk_gain, da_lambda, da_subln_gain,
           na_qk_gain, na_rpb, hg_lb_logits, hg_norm_gain, w_up, w_merge, w_out):
    batch, n, _ = x.shape
    n_ctx = ctx.shape[1]
    assert n == 2048 and n % GRID_W == 0 and batch + 1 <= MOD_ROWS

    cc = jnp.concatenate([c, c_ctx[None, :], jnp.zeros((MOD_ROWS - batch - 1, D_MODEL), F32)], axis=0)
    mod = _modulation(cc, w_mod, b_mod).reshape(DEPTH, MOD_ROWS, 3, D_MODEL)

    p_lb = jax.nn.softmax(hg_lb_logits.astype(F32), axis=1)
    lower = jnp.cumsum(p_lb, axis=1) - p_lb[:, :1]
    log_lb = jnp.log(jnp.maximum(lower, LB_FLOOR))
    log_1mlb = jnp.log1p(-lower)
    lv = da_lambda.astype(F32)
    lam_all = jnp.exp(jnp.sum(lv[:, 0] * lv[:, 1], axis=-1)) - jnp.exp(jnp.sum(lv[:, 2] * lv[:, 3], axis=-1))

    cos, sin = _rope_tables(n)
    cs_lat, cs_ctx = _dft_tables(n), _dft_tables(n_ctx)
    bc, bs = _channel_dft_tables()
    bias_all = _na_bias_tables(na_rpb)

    xl = x.reshape(batch * n, D_MODEL)
    xc = ctx.reshape(batch * n_ctx, D_MODEL)
    for l in range(DEPTH):
        need_ctx = l < DEPTH - 1
        lam_init = 0.8 - 0.6 * math.exp(-0.3 * l)
        lam = (lam_all[l] + lam_init).reshape(1)
        ng = norm_gain[l].reshape(1, D_MODEL)
        w = w_in[l].astype(BF16)
        gda = jnp.tile(da_qk_gain[l], (1, BR // DA_DH))
        gna = jnp.tile(na_qk_gain[l], (1, BR // NA_DH))
        lbp = jnp.stack([log_lb[0, l], log_1mlb[0, l], log_lb[1, l], log_1mlb[1, l]])
        subln = da_subln_gain[l].reshape(1, DA_DV)
        hgain = jnp.tile(hg_norm_gain[l].reshape(1, HG_D), (1, HG_HEADS))
        bias = bias_all[l]

        slab, g = _inproj(xl, mod[l], ng, w, gda, gna, lbp, cos, sin,
                          tm=512, rows_per_mod=n, mod_row0=0, rope=True)
        slab_c, g_c = _inproj(xc, mod[l], ng, w, gda, gna, lbp, cos[:n_ctx], sin[:n_ctx],
                              tm=n_ctx, rows_per_mod=batch * n_ctx, mod_row0=batch, rope=False)

        ya = _diff_attention(lam, subln, slab, [(slab, n), (slab_c, n_ctx)], nq=n, tq=512, lam_init=lam_init)
        yb = _neigh_attention(bias, slab, slab_c, n=n, n_ctx=n_ctx)
        yh, yh_c = _hgrn(hgain, slab, g, slab_c, g_c, n=n, n_ctx=n_ctx)
        yf = _fourier(cs_lat, bc, bs, slab, n=n)

        wup = w_up[l].astype(BF16)
        wmg = w_merge[l].astype(BF16)
        wout = w_out[l].astype(BF16)
        xl = _merge(xl, mod[l], ng, (ya, yb, yh, yf), slab, wup, wmg, wout,
                    tm=512, rows_per_mod=n, mod_row0=0)
        if need_ctx:
            ya_c = _diff_attention(lam, subln, slab_c, [(slab_c, n_ctx)], nq=n_ctx, tq=n_ctx, lam_init=lam_init)
            yb_c = _ctx_attention(slab_c, n_ctx=n_ctx)
            yf_c = _fourier(cs_ctx, bc, bs, slab_c, n=n_ctx)
            xc = _merge(xc, mod[l], ng, (ya_c, yb_c, yh_c, yf_c), slab_c, wup, wmg, wout,
                        tm=n_ctx, rows_per_mod=batch * n_ctx, mod_row0=batch)
    return xl.reshape(batch, n, D_MODEL)
```

```python
import functools
import math

import numpy as np
import jax
import jax.numpy as jnp
from jax import lax
from jax.experimental import pallas as pl
from jax.experimental.pallas import tpu as pltpu

D_MODEL = 1024
DEPTH = 4
GRID_W = 64
BR = 256
DA_HEADS, DA_DH, DA_DV = 4, 32, 64
NA_HEADS, NA_DH, NA_WIN_H, NA_WIN_W = 4, 64, 8, 16
HG_HEADS, HG_D, HG_CHUNK = 4, 64, 64
FT_DG = 64
IN_WIDTH = 15 * BR
LB_FLOOR = 1e-20
ROPE_THETA = 10000.0
EPS = 1e-6
NEG_INF = -1e30

COL_SG = 0
COL_QA, COL_KA, COL_QB, COL_KB = 4, 5, 6, 7
COL_VA, COL_VB = 4, 5
COL_HQ, COL_HI, COL_U = 12, 13, 14
SLAB_W = 15 * BR
LOG2E = 1.4426950408889634

NA_QROWS = 4
NA_KROWS = 12
MOD_ROWS = 24

VMEM_LIMIT_BYTES = 56 * 1024 * 1024

F32 = jnp.float32
BF16 = jnp.bfloat16


def _cparams(n_axes):
    return pltpu.CompilerParams(dimension_semantics=("arbitrary",) * n_axes, vmem_limit_bytes=VMEM_LIMIT_BYTES)


def _resident(shape):
    nd = len(shape)
    return pl.BlockSpec(shape, lambda *_: (0,) * nd, pipeline_mode=pl.Buffered(1))


def _dot(a, b):
    return jnp.dot(a, b, preferred_element_type=F32)


def _dot_nt(a, b):
    return lax.dot_general(a, b, (((1,), (1,)), ((), ())), preferred_element_type=F32)


def _dot_tn(a, b):
    return lax.dot_general(a, b, (((0,), (0,)), ((), ())), preferred_element_type=F32)


def _split3(v):
    hi = v.astype(BF16)
    r = v - hi.astype(F32)
    mid = r.astype(BF16)
    lo = (r - mid.astype(F32)).astype(BF16)
    return hi, mid, lo


def _group_mean(v, gsize, passes):
    w = v.shape[-1]
    r = lax.broadcasted_iota(jnp.int32, (w, w), 0) // gsize
    c = lax.broadcasted_iota(jnp.int32, (w, w), 1) // gsize
    ones = jnp.where(r == c, 1.0 / gsize, 0.0).astype(BF16)
    total, rest = None, v
    for _ in range(passes):
        piece = rest.astype(BF16)
        rest = rest - piece.astype(F32)
        part = _dot(piece, ones)
        total = part if total is None else total + part
    return total


def _group_rmsnorm(v, gain, gsize, passes=2):
    return v * lax.rsqrt(_group_mean(v * v, gsize, passes) + EPS) * gain


def _silu(v):
    return v * jax.nn.sigmoid(v)


def _mod_kernel(c_ref, w_ref, b_ref, o_ref):
    s = _silu(c_ref[...]).astype(BF16)
    o_ref[...] = _dot(s, w_ref[...].astype(BF16)) + b_ref[...]


def _modulation(cc, w_mod, b_mod):
    tn = D_MODEL
    return pl.pallas_call(
        _mod_kernel,
        grid=(DEPTH, 3 * D_MODEL // tn),
        in_specs=[
            pl.BlockSpec((MOD_ROWS, D_MODEL), lambda l, j: (0, 0)),
            pl.BlockSpec((None, D_MODEL, tn), lambda l, j: (l, 0, j)),
            pl.BlockSpec((None, 1, tn), lambda l, j: (l, 0, j)),
        ],
        out_specs=pl.BlockSpec((None, MOD_ROWS, tn), lambda l, j: (l, 0, j)),
        out_shape=jax.ShapeDtypeStruct((DEPTH, MOD_ROWS, 3 * D_MODEL), F32),
        compiler_params=_cparams(2),
        name="modulation",
    )(cc, w_mod, b_mod.reshape(DEPTH, 1, 3 * D_MODEL))


def _modulated_norm(x, mod_ref, ng_ref):
    ms = jnp.mean(x * x, axis=-1, keepdims=True)
    row_scale = ng_ref[...] * (1.0 + mod_ref[1:2, :])
    return x * lax.rsqrt(ms + EPS) * row_scale + mod_ref[0:1, :]


def _log_forget(f, lb_floored, one_minus_lb):
    return jnp.log(lb_floored + one_minus_lb * jax.nn.sigmoid(f))


def _rope(v, cos, sin_signed):
    lane = lax.broadcasted_iota(jnp.int32, v.shape, 1)
    first_half = (lane % DA_DH) < (DA_DH // 2)
    swapped = jnp.where(first_half, pltpu.roll(v, BR - DA_DH // 2, 1), pltpu.roll(v, DA_DH // 2, 1))
    return v * cos + swapped * sin_signed


def _inproj_kernel(x_ref, mod_ref, ng_ref, w_ref, gda_ref, gna_ref, lb_ref, cos_ref, sin_ref, slab_ref, g_ref, *, rope):
    h = _modulated_norm(x_ref[...], mod_ref, ng_ref).astype(BF16)

    def proj(col):
        return _dot(h, w_ref[:, col * BR:(col + 1) * BR])

    def put(col, v):
        slab_ref[:, col * BR:(col + 1) * BR] = v.astype(BF16)

    def put_values(col2, v):
        ones = jnp.ones((v.shape[0], 64), F32)
        pieces = []
        for hd in range(BR // 64):
            pieces += [v[:, hd * 64:(hd + 1) * 64], ones]
        slab_ref[:, col2 * 2 * BR:(col2 + 1) * 2 * BR] = jnp.concatenate(pieces, axis=-1).astype(BF16)

    g_ref[:, 0:BR] = _log_forget(proj(7), lb_ref[0:1, :], lb_ref[1:2, :])
    g_ref[:, BR:2 * BR] = _log_forget(proj(8), lb_ref[2:3, :], lb_ref[3:4, :])
    q = _group_rmsnorm(proj(0), gda_ref[0:1, :], DA_DH, passes=1)
    k = _group_rmsnorm(proj(1), gda_ref[1:2, :], DA_DH, passes=1)
    if rope:
        q = _rope(q, cos_ref[...], sin_ref[...])
        k = _rope(k, cos_ref[...], sin_ref[...])
    put(COL_QA, q)
    put(COL_KA, k)
    put(COL_QB, _group_rmsnorm(proj(3), gna_ref[0:1, :], NA_DH, passes=1))
    put(COL_KB, _group_rmsnorm(proj(4), gna_ref[1:2, :], NA_DH, passes=1))
    for i in range(4):
        put(COL_SG + i, _silu(proj(11 + i)))
    put(COL_HQ, _silu(proj(6)))
    put_values(COL_VA, proj(2))
    put_values(COL_VB, proj(5))
    put(COL_HI, proj(9))
    put(COL_U, proj(10))


def _inproj(x2d, mod_l, ng, w, gda, gna, lbp, cos, sin, *, tm, rows_per_mod, mod_row0, rope):
    n = x2d.shape[0]
    n_pos = cos.shape[0]
    tiles_per_seq = n_pos // tm

    def mod_map(t):
        return ((t * tm) // rows_per_mod + mod_row0, 0, 0)

    return pl.pallas_call(
        functools.partial(_inproj_kernel, rope=rope),
        grid=(n // tm,),
        in_specs=[
            pl.BlockSpec((tm, D_MODEL), lambda t: (t, 0)),
            pl.BlockSpec((None, 3, D_MODEL), mod_map),
            _resident((1, D_MODEL)),
            _resident((D_MODEL, IN_WIDTH)),
            _resident((2, BR)),
            _resident((2, BR)),
            _resident((4, BR)),
            pl.BlockSpec((tm, BR), lambda t: (t % tiles_per_seq, 0)),
            pl.BlockSpec((tm, BR), lambda t: (t % tiles_per_seq, 0)),
        ],
        out_specs=[
            pl.BlockSpec((tm, SLAB_W), lambda t: (t, 0)),
            pl.BlockSpec((tm, 2 * BR), lambda t: (t, 0)),
        ],
        out_shape=[
            jax.ShapeDtypeStruct((n, SLAB_W), BF16),
            jax.ShapeDtypeStruct((n, 2 * BR), F32),
        ],
        compiler_params=_cparams(1),
        name="inproj_rope" if rope else "inproj",
    )(x2d, mod_l, ng, w, gda, gna, lbp, cos, sin)


def _softmax_pv_scores(scores, values):
    m = scores[0].max(axis=-1, keepdims=True)
    for s in scores[1:]:
        m = jnp.maximum(m, s.max(axis=-1, keepdims=True))
    acc = None
    for s, v in zip(scores, values):
        p = _dot(jnp.exp2(s - m).astype(BF16), v)
        acc = p if acc is None else acc + p
    return (acc / pltpu.roll(acc, 64, 1))[:, :64]


def _softmax_pv(q, segments):
    return _softmax_pv_scores([_dot_nt(q, k) for k, _ in segments], [v for _, v in segments])


def _da_kernel(lam_ref, g_ref, q_ref, *refs, lam_init, n_seg):
    kv_refs, o_ref = refs[:2 * n_seg], refs[2 * n_seg]
    lam = lam_ref[0]
    outs = []
    for h in range(DA_HEADS):
        vs = slice(h * 128, (h + 1) * 128)
        att = []
        for j in range(2):
            qs = slice((2 * h + j) * DA_DH, (2 * h + j + 1) * DA_DH)
            segs = [(kv_refs[2 * s][:, qs], kv_refs[2 * s + 1][:, vs]) for s in range(n_seg)]
            att.append(_softmax_pv(q_ref[:, qs], segs))
        o = att[0] - lam * att[1]
        o = o * lax.rsqrt(jnp.mean(o * o, axis=-1, keepdims=True) + EPS) * g_ref[...] * (1.0 - lam_init)
        outs.append(o)
    o_ref[...] = jnp.concatenate(outs, axis=-1).astype(BF16)


def _diff_attention(lam, subln, slab_q, key_slabs, *, nq, tq, lam_init):
    batch = slab_q.shape[0] // nq
    qb = nq // tq
    in_specs = [
        pl.BlockSpec(memory_space=pltpu.SMEM),
        _resident((1, DA_DV)),
        pl.BlockSpec((tq, BR), lambda b, i: (b * qb + i, COL_QA)),
    ]
    args = [lam, subln, slab_q]
    for slab, nk in key_slabs:
        in_specs += [pl.BlockSpec((nk, BR), lambda b, i: (b, COL_KA)),
                     pl.BlockSpec((nk, 2 * BR), lambda b, i: (b, COL_VA))]
        args += [slab, slab]
    return pl.pallas_call(
        functools.partial(_da_kernel, lam_init=lam_init, n_seg=len(key_slabs)),
        grid=(batch, qb),
        in_specs=in_specs,
        out_specs=pl.BlockSpec((tq, BR), lambda b, i: (b * qb + i, 0)),
        out_shape=jax.ShapeDtypeStruct((batch * nq, BR), BF16),
        compiler_params=_cparams(2),
        name="diff_attention",
    )(*args)


def _na_bias_tables(rpb):
    rows = 2048 // GRID_W
    n_dr, n_dc = 2 * NA_WIN_H - 1, 2 * NA_WIN_W - 1
    geoms = [(0, 0), (NA_QROWS, 0), (rows - NA_QROWS, rows - NA_KROWS)]
    qc = np.arange(GRID_W)[:, None]
    kc = np.arange(GRID_W)[None, :]
    c0 = np.clip(qc - NA_WIN_W // 2, 0, GRID_W - NA_WIN_W)
    col_ok = (kc >= c0) & (kc < c0 + NA_WIN_W)
    dc = np.clip(kc - qc, 1 - NA_WIN_W, NA_WIN_W - 1) + NA_WIN_W - 1
    col_onehot = (dc[None] == np.arange(n_dc)[:, None, None]).astype(np.float32)
    qr = np.arange(NA_QROWS)[:, None]
    kr = np.arange(NA_KROWS)[None, :]
    row_onehot = np.zeros((3, NA_QROWS, NA_KROWS, n_dr), np.float32)
    ok = np.zeros((3, NA_QROWS, GRID_W, NA_KROWS, GRID_W), bool)
    for gi, (q0, ws) in enumerate(geoms):
        r = q0 + qr
        r0 = np.clip(r - NA_WIN_H // 2, 0, rows - NA_WIN_H)
        kabs = ws + kr
        row_ok = (kabs >= r0) & (kabs < r0 + NA_WIN_H)
        dr = kabs - r + NA_WIN_H - 1
        row_onehot[gi] = (dr[..., None] == np.arange(n_dr)) & row_ok[..., None]
        ok[gi] = row_ok[:, None, :, None] & col_ok[None, :, None, :]
    by_col = jnp.einsum("lhrd,dqc->lhrqc", rpb.astype(F32), col_onehot, precision=lax.Precision.HIGHEST)
    bias = jnp.einsum("gakr,lhrqc->lghaqkc", row_onehot, by_col, precision=lax.Precision.HIGHEST)
    bias = jnp.where(ok[None, :, None], bias * LOG2E, NEG_INF)
    return bias.reshape(rpb.shape[0], 3, NA_HEADS, NA_QROWS * GRID_W, NA_KROWS * GRID_W)


def _na_kernel(bias_ref, q_ref, kl_ref, vl_ref, kc_ref, vc_ref, o_ref):
    i = pl.program_id(1)
    n_steps = pl.num_programs(1)
    last_ws = 2048 // GRID_W - NA_KROWS
    ws = pl.multiple_of(jnp.clip(NA_QROWS * i - NA_WIN_H // 2, 0, last_ws) * GRID_W, NA_QROWS * GRID_W)
    geom = jnp.where(i == 0, 0, jnp.where(i == n_steps - 1, 2, 1))
    nkw = NA_KROWS * GRID_W
    outs = []
    for h in range(NA_HEADS):
        hs = slice(h * NA_DH, (h + 1) * NA_DH)
        vs = slice(h * 128, (h + 1) * 128)
        q = q_ref[:, hs]
        s_w = _dot_nt(q, kl_ref[pl.ds(ws, nkw), hs]) + bias_ref[geom, h]
        s_c = _dot_nt(q, kc_ref[:, hs])
        outs.append(_softmax_pv_scores([s_w, s_c], [vl_ref[pl.ds(ws, nkw), vs], vc_ref[:, vs]]))
    o_ref[...] = jnp.concatenate(outs, axis=-1).astype(BF16)


def _neigh_attention(bias, slab, slab_c, *, n, n_ctx):
    batch = slab.shape[0] // n
    tq = NA_QROWS * GRID_W
    steps = n // tq
    return pl.pallas_call(
        _na_kernel,
        grid=(batch, steps),
        in_specs=[
            _resident(bias.shape),
            pl.BlockSpec((tq, BR), lambda b, i: (b * steps + i, COL_QB)),
            pl.BlockSpec((n, BR), lambda b, i: (b, COL_KB)),
            pl.BlockSpec((n, 2 * BR), lambda b, i: (b, COL_VB)),
            pl.BlockSpec((n_ctx, BR), lambda b, i: (b, COL_KB)),
            pl.BlockSpec((n_ctx, 2 * BR), lambda b, i: (b, COL_VB)),
        ],
        out_specs=pl.BlockSpec((tq, BR), lambda b, i: (b * steps + i, 0)),
        out_shape=jax.ShapeDtypeStruct((batch * n, BR), BF16),
        compiler_params=_cparams(2),
        name="neigh_attention",
    )(bias, slab, slab, slab, slab_c, slab_c)


def _ctx_attn_kernel(q_ref, k_ref, v_ref, o_ref):
    outs = []
    for h in range(NA_HEADS):
        hs = slice(h * NA_DH, (h + 1) * NA_DH)
        outs.append(_softmax_pv(q_ref[:, hs], [(k_ref[:, hs], v_ref[:, h * 128:(h + 1) * 128])]))
    o_ref[...] = jnp.concatenate(outs, axis=-1).astype(BF16)


def _ctx_attention(slab_c, *, n_ctx):
    batch = slab_c.shape[0] // n_ctx
    return pl.pallas_call(
        _ctx_attn_kernel,
        grid=(batch,),
        in_specs=[
            pl.BlockSpec((n_ctx, BR), lambda b: (b, COL_QB)),
            pl.BlockSpec((n_ctx, BR), lambda b: (b, COL_KB)),
            pl.BlockSpec((n_ctx, 2 * BR), lambda b: (b, COL_VB)),
        ],
        out_specs=pl.BlockSpec((n_ctx, BR), lambda b: (b, 0)),
        out_shape=jax.ShapeDtypeStruct((batch * n_ctx, BR), BF16),
        compiler_params=_cparams(1),
        name="ctx_attention",
    )(slab_c, slab_c, slab_c)


def _row_block_broadcast(b, size, offset):
    c = b.shape[0]
    if size >= 8:
        parts = [jnp.broadcast_to(b[s + offset:s + offset + 1, :], (size, b.shape[1])) for s in range(0, c, size)]
        return parts[0] if len(parts) == 1 else jnp.concatenate(parts, axis=0)
    pos = lax.broadcasted_iota(jnp.int32, b.shape, 0) % size
    out = b
    for p in range(size):
        if p != offset:
            out = jnp.where(pos == p, pltpu.roll(b, (p - offset) % c, 0), out)
    return out


def _hgrn_chunk(q, v, g, st_ref, d, gain_unused=None):
    c = HG_CHUNK
    rev = d == 1
    t_sq = lax.broadcasted_iota(jnp.int32, (c, c), 0)
    u_sq = lax.broadcasted_iota(jnp.int32, (c, c), 1)
    cum = jnp.where((u_sq >= t_sq) if rev else (u_sq <= t_sq), 1.0, 0.0).astype(BF16)
    g3 = _split3(g)
    b = _dot(cum, g3[0]) + _dot(cum, g3[1]) + _dot(cum, g3[2])
    btot = b[0:1, :] if rev else b[c - 1:c, :]
    k = 1.0 - jnp.exp(g)

    row = lax.broadcasted_iota(jnp.int32, (c, BR), 0)
    t4 = lax.broadcasted_iota(jnp.int32, (c, 4 * c), 0)
    s4 = lax.broadcasted_iota(jnp.int32, (c, 4 * c), 1) % c
    rb = lax.broadcasted_iota(jnp.int32, (4 * c, BR), 0) // c
    cb = lax.broadcasted_iota(jnp.int32, (4 * c, BR), 1) // HG_D
    head_diag = rb == cb

    def heads_block_diag(x):
        return jnp.where(head_diag, jnp.concatenate([x] * HG_HEADS, axis=0), 0.0).astype(BF16)

    att = jnp.where(t4 == s4, _dot_nt((q).astype(BF16), heads_block_diag(k)), 0.0)
    m = c // 2
    while m >= 1:
        ref_b = _row_block_broadcast(b, 2 * m, m if rev else m - 1)
        pos = row % (2 * m)
        q_side = (pos < m) if rev else (pos >= m)
        qm = jnp.where(q_side, q * jnp.exp(jnp.minimum(b - ref_b, 0.0)), 0.0)
        km = jnp.where(q_side, 0.0, k * jnp.exp(jnp.minimum(ref_b - b, 0.0)))
        same_block = (t4 // (2 * m)) == (s4 // (2 * m))
        att = att + jnp.where(same_block, _dot_nt(qm.astype(BF16), heads_block_diag(km)), 0.0)
        m //= 2

    st = st_ref[d]
    o = _dot(att.astype(BF16), heads_block_diag(v)) + _dot_nt((q * jnp.exp(b)).astype(BF16), st.astype(BF16))
    k_st = k * jnp.exp(btot - b)
    r2 = lax.broadcasted_iota(jnp.int32, (BR, BR), 0) // HG_D
    c2 = lax.broadcasted_iota(jnp.int32, (BR, BR), 1) // HG_D
    upd = jnp.where(r2 == c2, _dot_tn(v.astype(BF16), k_st.astype(BF16)), 0.0)
    st_ref[d] = st * jnp.exp(btot) + upd
    return o


def _hgrn_kernel(gain_ref, ql_ref, il_ref, gl_ref, qc_ref, ic_ref, gc_ref, yl_ref, yc_ref, st_ref, ol_ref, oc_ref):
    c = HG_CHUNK
    st_ref[...] = jnp.zeros(st_ref.shape, F32)

    def scan(q_ref, i_ref, g_ref, o_ref):
        n_chunks = q_ref.shape[0] // c

        def body(step, carry):
            for d in range(2):
                idx = step if d == 0 else n_chunks - 1 - step
                rows = pl.ds(pl.multiple_of(idx * c, c), c)
                o = _hgrn_chunk(q_ref[rows, :].astype(F32), i_ref[rows, :].astype(F32),
                                g_ref[rows, d * BR:(d + 1) * BR], st_ref, d)
                o_ref[d, rows, :] = o
            return carry

        lax.fori_loop(0, n_chunks, body, 0)

    scan(qc_ref, ic_ref, gc_ref, oc_ref)
    scan(ql_ref, il_ref, gl_ref, ol_ref)
    yl_ref[...] = _group_rmsnorm(ol_ref[0] + ol_ref[1], gain_ref[...], HG_D).astype(BF16)
    yc_ref[...] = _group_rmsnorm(oc_ref[0] + oc_ref[1], gain_ref[...], HG_D).astype(BF16)


def _hgrn(gain, slab, g, slab_c, g_c, *, n, n_ctx):
    batch = slab.shape[0] // n
    return pl.pallas_call(
        _hgrn_kernel,
        grid=(batch,),
        in_specs=[
            _resident((1, BR)),
            pl.BlockSpec((n, BR), lambda b: (b, COL_HQ)),
            pl.BlockSpec((n, BR), lambda b: (b, COL_HI)),
            pl.BlockSpec((n, 2 * BR), lambda b: (b, 0)),
            pl.BlockSpec((n_ctx, BR), lambda b: (b, COL_HQ)),
            pl.BlockSpec((n_ctx, BR), lambda b: (b, COL_HI)),
            pl.BlockSpec((n_ctx, 2 * BR), lambda b: (b, 0)),
        ],
        out_specs=[
            pl.BlockSpec((n, BR), lambda b: (b, 0)),
            pl.BlockSpec((n_ctx, BR), lambda b: (b, 0)),
        ],
        out_shape=[
            jax.ShapeDtypeStruct((batch * n, BR), BF16),
            jax.ShapeDtypeStruct((batch * n_ctx, BR), BF16),
        ],
        scratch_shapes=[
            pltpu.VMEM((2, BR, BR), F32),
            pltpu.VMEM((2, n, BR), F32),
            pltpu.VMEM((2, n_ctx, BR), F32),
        ],
        compiler_params=_cparams(1),
        name="hgrn2",
    )(gain, slab, slab, g, slab_c, slab_c, g_c)


def _dft_tables(n):
    lo = 64
    hi = n // lo
    kk = np.arange(n, dtype=np.int64)
    ang_hi = 2.0 * np.pi * ((np.arange(hi, dtype=np.int64)[:, None] * lo * kk[None, :]) % n) / n
    ang_lo = 2.0 * np.pi * ((np.arange(lo, dtype=np.int64)[:, None] * kk[None, :]) % n) / n
    ch, sh = (jnp.asarray(f(ang_hi), F32)[:, None, :] for f in (np.cos, np.sin))
    cl, sl = (jnp.asarray(f(ang_lo), F32)[None, :, :] for f in (np.cos, np.sin))
    scale = 1.0 / math.sqrt(n)
    cos = ((ch * cl - sh * sl) * scale).reshape(n, n)
    sin = ((sh * cl + ch * sl) * scale).reshape(n, n)
    return jnp.concatenate([cos, -sin], axis=1).astype(BF16)


def _channel_dft_tables():
    j = np.arange(BR)
    same = (j[:, None] // FT_DG) == (j[None, :] // FT_DG)
    ang = 2.0 * np.pi * (((j[:, None] % FT_DG) * (j[None, :] % FT_DG)) % FT_DG) / FT_DG
    scale = 1.0 / math.sqrt(FT_DG)
    return (jnp.asarray(np.where(same, np.cos(ang), 0.0) * scale, BF16),
            jnp.asarray(np.where(same, np.sin(ang), 0.0) * scale, BF16))


def _fourier_kernel(cs_ref, bc_ref, bs_ref, u_ref, o_ref):
    u = u_ref[...]
    stacked = jnp.concatenate([_dot(u, bc_ref[...]).astype(BF16), _dot(u, bs_ref[...]).astype(BF16)], axis=0)
    o_ref[...] = _dot(cs_ref[...], stacked).astype(BF16)


def _fourier(cs, bc, bs, slab, *, n):
    batch = slab.shape[0] // n
    return pl.pallas_call(
        _fourier_kernel,
        grid=(batch,),
        in_specs=[
            _resident((n, 2 * n)),
            _resident((BR, BR)),
            _resident((BR, BR)),
            pl.BlockSpec((n, BR), lambda b: (b, COL_U)),
        ],
        out_specs=pl.BlockSpec((n, BR), lambda b: (b, 0)),
        out_shape=jax.ShapeDtypeStruct((batch * n, BR), BF16),
        compiler_params=_cparams(1),
        name="fourier",
    )(cs, bc, bs, slab)


def _merge_kernel(x_ref, mod_ref, ng_ref, ya_ref, yb_ref, yh_ref, yf_ref, sg_ref, wup_ref, wmg_ref, wout_ref, o_ref):
    x = x_ref[...]
    h = _modulated_norm(x, mod_ref, ng_ref).astype(BF16)
    acc = None
    for i, y_ref in enumerate((ya_ref, yb_ref, yh_ref, yf_ref)):
        y = (y_ref[...].astype(F32) * sg_ref[:, i * BR:(i + 1) * BR].astype(F32)).astype(BF16)
        term = jax.nn.sigmoid(_dot(h, wmg_ref[i])) * _dot(y, wup_ref[i])
        acc = term if acc is None else acc + term
    o_ref[...] = x + mod_ref[2:3, :] * _dot(acc.astype(BF16), wout_ref[...])


def _merge(x2d, mod_l, ng, ys, slab, wup, wmg, wout, *, tm, rows_per_mod, mod_row0):
    n = x2d.shape[0]

    def mod_map(t):
        return ((t * tm) // rows_per_mod + mod_row0, 0, 0)

    return pl.pallas_call(
        _merge_kernel,
        grid=(n // tm,),
        in_specs=[
            pl.BlockSpec((tm, D_MODEL), lambda t: (t, 0)),
            pl.BlockSpec((None, 3, D_MODEL), mod_map),
            _resident((1, D_MODEL)),
            *[pl.BlockSpec((tm, BR), lambda t: (t, 0)) for _ in range(4)],
            pl.BlockSpec((tm, 4 * BR), lambda t: (t, COL_SG)),
            _resident((4, BR, D_MODEL)),
            _resident((4, D_MODEL, D_MODEL)),
            _resident((D_MODEL, D_MODEL)),
        ],
        out_specs=pl.BlockSpec((tm, D_MODEL), lambda t: (t, 0)),
        out_shape=jax.ShapeDtypeStruct((n, D_MODEL), F32),
        compiler_params=_cparams(1),
        name="merge",
    )(x2d, mod_l, ng, *ys, slab, wup, wmg, wout)


def _rope_tables(n):
    t = jnp.arange(n)
    row = (t // GRID_W).astype(F32)
    col = (t % GRID_W).astype(F32)
    d_axis = DA_DH // 2
    inv = ROPE_THETA ** (-jnp.arange(0, d_axis, 2, dtype=F32) / d_axis)
    ang = jnp.concatenate([row[:, None] * inv, col[:, None] * inv], axis=-1)
    cos, sin = jnp.cos(ang), jnp.sin(ang)
    cos_h = jnp.concatenate([cos, cos], axis=-1)
    sin_h = jnp.concatenate([-sin, sin], axis=-1)
    return jnp.tile(cos_h, (1, 2 * DA_HEADS)), jnp.tile(sin_h, (1, 2 * DA_HEADS))


def kernel(x, c, ctx, c_ctx, norm_gain, w_mod, b_mod, w_in, da_qk_gain, da_lambda, da_subln_gain,
           na_qk_gain, na_rpb, hg_lb_logits, hg_norm_gain, w_up, w_merge, w_out):
    batch, n, _ = x.shape
    n_ctx = ctx.shape[1]
    assert n == 2048 and n % GRID_W == 0 and batch + 1 <= MOD_ROWS

    cc = jnp.concatenate([c, c_ctx[None, :], jnp.zeros((MOD_ROWS - batch - 1, D_MODEL), F32)], axis=0)
    mod = _modulation(cc, w_mod, b_mod).reshape(DEPTH, MOD_ROWS, 3, D_MODEL)

    p_lb = jax.nn.softmax(hg_lb_logits.astype(F32), axis=1)
    lower = jnp.cumsum(p_lb, axis=1) - p_lb[:, :1]
    lb_floored = jnp.maximum(lower, LB_FLOOR)
    one_minus_lb = 1.0 - lower
    lv = da_lambda.astype(F32)
    lam_all = jnp.exp(jnp.sum(lv[:, 0] * lv[:, 1], axis=-1)) - jnp.exp(jnp.sum(lv[:, 2] * lv[:, 3], axis=-1))

    cos, sin = _rope_tables(n)
    cs_lat, cs_ctx = _dft_tables(n), _dft_tables(n_ctx)
    bc, bs = _channel_dft_tables()
    bias_all = _na_bias_tables(na_rpb)

    xl = x.reshape(batch * n, D_MODEL)
    xc = ctx.reshape(batch * n_ctx, D_MODEL)
    for l in range(DEPTH):
        need_ctx = l < DEPTH - 1
        lam_init = 0.8 - 0.6 * math.exp(-0.3 * l)
        lam = (lam_all[l] + lam_init).reshape(1)
        ng = norm_gain[l].reshape(1, D_MODEL)
        w = w_in[l].astype(BF16)
        gda = jnp.tile(da_qk_gain[l], (1, BR // DA_DH)) * jnp.array([[DA_DH ** -0.5 * LOG2E], [1.0]], F32)
        gna = jnp.tile(na_qk_gain[l], (1, BR // NA_DH)) * jnp.array([[NA_DH ** -0.5 * LOG2E], [1.0]], F32)
        lbp = jnp.stack([lb_floored[0, l], one_minus_lb[0, l], lb_floored[1, l], one_minus_lb[1, l]])
        subln = da_subln_gain[l].reshape(1, DA_DV)
        hgain = jnp.tile(hg_norm_gain[l].reshape(1, HG_D), (1, HG_HEADS))
        bias = bias_all[l]

        slab, g = _inproj(xl, mod[l], ng, w, gda, gna, lbp, cos, sin,
                          tm=512, rows_per_mod=n, mod_row0=0, rope=True)
        slab_c, g_c = _inproj(xc, mod[l], ng, w, gda, gna, lbp, cos[:n_ctx], sin[:n_ctx],
                              tm=n_ctx, rows_per_mod=batch * n_ctx, mod_row0=batch, rope=False)

        ya = _diff_attention(lam, subln, slab, [(slab, n), (slab_c, n_ctx)], nq=n, tq=512, lam_init=lam_init)
        yb = _neigh_attention(bias, slab, slab_c, n=n, n_ctx=n_ctx)
        yh, yh_c = _hgrn(hgain, slab, g, slab_c, g_c, n=n, n_ctx=n_ctx)
        yf = _fourier(cs_lat, bc, bs, slab, n=n)

        wup = w_up[l].astype(BF16)
        wmg = w_merge[l].astype(BF16)
        wout = w_out[l].astype(BF16)
        xl = _merge(xl, mod[l], ng, (ya, yb, yh, yf), slab, wup, wmg, wout,
                    tm=512, rows_per_mod=n, mod_row0=0)
        if need_ctx:
            ya_c = _diff_attention(lam, subln, slab_c, [(slab_c, n_ctx)], nq=n_ctx, tq=n_ctx, lam_init=lam_init)
            yb_c = _ctx_attention(slab_c, n_ctx=n_ctx)
            yf_c = _fourier(cs_ctx, bc, bs, slab_c, n=n_ctx)
            xc = _merge(xc, mod[l], ng, (ya_c, yb_c, yh_c, yf_c), slab_c, wup, wmg, wout,
                        tm=n_ctx, rows_per_mod=batch * n_ctx, mod_row0=batch)
    return xl.reshape(batch, n, D_MODEL)
```

```python
import functools
import math

import numpy as np
import jax
import jax.numpy as jnp
from jax import lax
from jax.experimental import pallas as pl
from jax.experimental.pallas import tpu as pltpu

D_MODEL = 1024
DEPTH = 4
GRID_W = 64
BR = 256
DA_HEADS, DA_DH, DA_DV = 4, 32, 64
NA_HEADS, NA_DH, NA_WIN_H, NA_WIN_W = 4, 64, 8, 16
HG_HEADS, HG_D, HG_CHUNK = 4, 64, 64
FT_DG = 64
IN_WIDTH = 15 * BR
LB_FLOOR = 1e-20
ROPE_THETA = 10000.0
EPS = 1e-6
NEG_INF = -1e30

COL_SG = 0
COL_QA, COL_KA, COL_QB, COL_KB = 4, 5, 6, 7
COL_VA, COL_VB = 4, 5
COL_HQ, COL_HI, COL_U = 12, 13, 14
SLAB_W = 15 * BR
LOG2E = 1.4426950408889634

NA_QROWS = 4
NA_KROWS = 12
MOD_ROWS = 24

VMEM_LIMIT_BYTES = 56 * 1024 * 1024

F32 = jnp.float32
BF16 = jnp.bfloat16


def _cparams(n_axes):
    return pltpu.CompilerParams(dimension_semantics=("arbitrary",) * n_axes, vmem_limit_bytes=VMEM_LIMIT_BYTES)


def _resident(shape):
    nd = len(shape)
    return pl.BlockSpec(shape, lambda *_: (0,) * nd, pipeline_mode=pl.Buffered(1))


def _dot(a, b):
    return jnp.dot(a, b, preferred_element_type=F32)


def _dot_nt(a, b):
    return lax.dot_general(a, b, (((1,), (1,)), ((), ())), preferred_element_type=F32)


def _dot_tn(a, b):
    return lax.dot_general(a, b, (((0,), (0,)), ((), ())), preferred_element_type=F32)


def _split3(v):
    hi = v.astype(BF16)
    r = v - hi.astype(F32)
    mid = r.astype(BF16)
    lo = (r - mid.astype(F32)).astype(BF16)
    return hi, mid, lo


def _group_mean(v, gsize, passes):
    w = v.shape[-1]
    r = lax.broadcasted_iota(jnp.int32, (w, w), 0) // gsize
    c = lax.broadcasted_iota(jnp.int32, (w, w), 1) // gsize
    ones = jnp.where(r == c, 1.0 / gsize, 0.0).astype(BF16)
    total, rest = None, v
    for _ in range(passes):
        piece = rest.astype(BF16)
        rest = rest - piece.astype(F32)
        part = _dot(piece, ones)
        total = part if total is None else total + part
    return total


def _group_rmsnorm(v, gain, gsize, passes=2):
    return v * lax.rsqrt(_group_mean(v * v, gsize, passes) + EPS) * gain


def _silu(v):
    return v * jax.nn.sigmoid(v)


def _mod_kernel(c_ref, w_ref, b_ref, o_ref):
    s = _silu(c_ref[...]).astype(BF16)
    o_ref[...] = _dot(s, w_ref[...].astype(BF16)) + b_ref[...]


def _modulation(cc, w_mod, b_mod):
    tn = D_MODEL
    return pl.pallas_call(
        _mod_kernel,
        grid=(DEPTH, 3 * D_MODEL // tn),
        in_specs=[
            pl.BlockSpec((MOD_ROWS, D_MODEL), lambda l, j: (0, 0)),
            pl.BlockSpec((None, D_MODEL, tn), lambda l, j: (l, 0, j)),
            pl.BlockSpec((None, 1, tn), lambda l, j: (l, 0, j)),
        ],
        out_specs=pl.BlockSpec((None, MOD_ROWS, tn), lambda l, j: (l, 0, j)),
        out_shape=jax.ShapeDtypeStruct((DEPTH, MOD_ROWS, 3 * D_MODEL), F32),
        compiler_params=_cparams(2),
        name="modulation",
    )(cc, w_mod, b_mod.reshape(DEPTH, 1, 3 * D_MODEL))


def _modulated_norm(x, mod_ref, ng_ref):
    ms = jnp.mean(x * x, axis=-1, keepdims=True)
    row_scale = ng_ref[...] * (1.0 + mod_ref[1:2, :])
    return x * lax.rsqrt(ms + EPS) * row_scale + mod_ref[0:1, :]


def _log_forget(f, lb_floored, one_minus_lb):
    return jnp.log(lb_floored + one_minus_lb * jax.nn.sigmoid(f))


def _rope(v, cos, sin_signed):
    lane = lax.broadcasted_iota(jnp.int32, v.shape, 1)
    first_half = (lane % DA_DH) < (DA_DH // 2)
    swapped = jnp.where(first_half, pltpu.roll(v, BR - DA_DH // 2, 1), pltpu.roll(v, DA_DH // 2, 1))
    return v * cos + swapped * sin_signed


def _inproj_kernel(x_ref, mod_ref, ng_ref, w_ref, gda_ref, gna_ref, lb_ref, cos_ref, sin_ref, slab_ref, g_ref, *, rope):
    h = _modulated_norm(x_ref[...], mod_ref, ng_ref).astype(BF16)

    def proj(col):
        return _dot(h, w_ref[:, col * BR:(col + 1) * BR])

    def put(col, v):
        slab_ref[:, col * BR:(col + 1) * BR] = v.astype(BF16)

    def put_values(col2, v):
        ones = jnp.ones((v.shape[0], 64), F32)
        pieces = []
        for hd in range(BR // 64):
            pieces += [v[:, hd * 64:(hd + 1) * 64], ones]
        slab_ref[:, col2 * 2 * BR:(col2 + 1) * 2 * BR] = jnp.concatenate(pieces, axis=-1).astype(BF16)

    g_ref[:, 0:BR] = _log_forget(proj(7), lb_ref[0:1, :], lb_ref[1:2, :])
    g_ref[:, BR:2 * BR] = _log_forget(proj(8), lb_ref[2:3, :], lb_ref[3:4, :])
    q = _group_rmsnorm(proj(0), gda_ref[0:1, :], DA_DH, passes=1)
    k = _group_rmsnorm(proj(1), gda_ref[1:2, :], DA_DH, passes=1)
    if rope:
        q = _rope(q, cos_ref[...], sin_ref[...])
        k = _rope(k, cos_ref[...], sin_ref[...])
    put(COL_QA, q)
    put(COL_KA, k)
    put(COL_QB, _group_rmsnorm(proj(3), gna_ref[0:1, :], NA_DH, passes=1))
    put(COL_KB, _group_rmsnorm(proj(4), gna_ref[1:2, :], NA_DH, passes=1))
    for i in range(4):
        put(COL_SG + i, _silu(proj(11 + i)))
    put(COL_HQ, _silu(proj(6)))
    put_values(COL_VA, proj(2))
    put_values(COL_VB, proj(5))
    put(COL_HI, proj(9))
    put(COL_U, proj(10))


def _inproj(x2d, mod_l, ng, w, gda, gna, lbp, cos, sin, *, tm, rows_per_mod, mod_row0, rope):
    n = x2d.shape[0]
    n_pos = cos.shape[0]
    tiles_per_seq = n_pos // tm

    def mod_map(t):
        return ((t * tm) // rows_per_mod + mod_row0, 0, 0)

    return pl.pallas_call(
        functools.partial(_inproj_kernel, rope=rope),
        grid=(n // tm,),
        in_specs=[
            pl.BlockSpec((tm, D_MODEL), lambda t: (t, 0)),
            pl.BlockSpec((None, 3, D_MODEL), mod_map),
            _resident((1, D_MODEL)),
            _resident((D_MODEL, IN_WIDTH)),
            _resident((2, BR)),
            _resident((2, BR)),
            _resident((4, BR)),
            pl.BlockSpec((tm, BR), lambda t: (t % tiles_per_seq, 0)),
            pl.BlockSpec((tm, BR), lambda t: (t % tiles_per_seq, 0)),
        ],
        out_specs=[
            pl.BlockSpec((tm, SLAB_W), lambda t: (t, 0)),
            pl.BlockSpec((tm, 2 * BR), lambda t: (t, 0)),
        ],
        out_shape=[
            jax.ShapeDtypeStruct((n, SLAB_W), BF16),
            jax.ShapeDtypeStruct((n, 2 * BR), F32),
        ],
        compiler_params=_cparams(1),
        name="inproj_rope" if rope else "inproj",
    )(x2d, mod_l, ng, w, gda, gna, lbp, cos, sin)


def _softmax_pv_scores(scores, values):
    m = scores[0].max(axis=-1, keepdims=True)
    for s in scores[1:]:
        m = jnp.maximum(m, s.max(axis=-1, keepdims=True))
    acc = None
    for s, v in zip(scores, values):
        p = _dot(jnp.exp2(s - m).astype(BF16), v)
        acc = p if acc is None else acc + p
    return (acc / pltpu.roll(acc, 64, 1))[:, :64]


def _softmax_pv(q, segments):
    return _softmax_pv_scores([_dot_nt(q, k) for k, _ in segments], [v for _, v in segments])


def _da_kernel(lam_ref, g_ref, q_ref, *refs, lam_init, n_seg):
    kv_refs, o_ref = refs[:2 * n_seg], refs[2 * n_seg]
    lam = lam_ref[0]
    outs = []
    for h in range(DA_HEADS):
        vs = slice(h * 128, (h + 1) * 128)
        att = []
        for j in range(2):
            qs = slice((2 * h + j) * DA_DH, (2 * h + j + 1) * DA_DH)
            segs = [(kv_refs[2 * s][:, qs], kv_refs[2 * s + 1][:, vs]) for s in range(n_seg)]
            att.append(_softmax_pv(q_ref[:, qs], segs))
        o = att[0] - lam * att[1]
        o = o * lax.rsqrt(jnp.mean(o * o, axis=-1, keepdims=True) + EPS) * g_ref[...] * (1.0 - lam_init)
        outs.append(o)
    o_ref[...] = jnp.concatenate(outs, axis=-1).astype(BF16)


def _diff_attention(lam, subln, slab_q, key_slabs, *, nq, tq, lam_init):
    batch = slab_q.shape[0] // nq
    qb = nq // tq
    in_specs = [
        pl.BlockSpec(memory_space=pltpu.SMEM),
        _resident((1, DA_DV)),
        pl.BlockSpec((tq, BR), lambda b, i: (b * qb + i, COL_QA)),
    ]
    args = [lam, subln, slab_q]
    for slab, nk in key_slabs:
        in_specs += [pl.BlockSpec((nk, BR), lambda b, i: (b, COL_KA)),
                     pl.BlockSpec((nk, 2 * BR), lambda b, i: (b, COL_VA))]
        args += [slab, slab]
    return pl.pallas_call(
        functools.partial(_da_kernel, lam_init=lam_init, n_seg=len(key_slabs)),
        grid=(batch, qb),
        in_specs=in_specs,
        out_specs=pl.BlockSpec((tq, BR), lambda b, i: (b * qb + i, 0)),
        out_shape=jax.ShapeDtypeStruct((batch * nq, BR), BF16),
        compiler_params=_cparams(2),
        name="diff_attention",
    )(*args)


def _na_bias_tables(rpb):
    rows = 2048 // GRID_W
    n_dr, n_dc = 2 * NA_WIN_H - 1, 2 * NA_WIN_W - 1
    geoms = [(0, 0), (NA_QROWS, 0), (rows - NA_QROWS, rows - NA_KROWS)]
    qc = np.arange(GRID_W)[:, None]
    kc = np.arange(GRID_W)[None, :]
    c0 = np.clip(qc - NA_WIN_W // 2, 0, GRID_W - NA_WIN_W)
    col_ok = (kc >= c0) & (kc < c0 + NA_WIN_W)
    dc = np.clip(kc - qc, 1 - NA_WIN_W, NA_WIN_W - 1) + NA_WIN_W - 1
    col_onehot = (dc[None] == np.arange(n_dc)[:, None, None]).astype(np.float32)
    qr = np.arange(NA_QROWS)[:, None]
    kr = np.arange(NA_KROWS)[None, :]
    row_onehot = np.zeros((3, NA_QROWS, NA_KROWS, n_dr), np.float32)
    ok = np.zeros((3, NA_QROWS, GRID_W, NA_KROWS, GRID_W), bool)
    for gi, (q0, ws) in enumerate(geoms):
        r = q0 + qr
        r0 = np.clip(r - NA_WIN_H // 2, 0, rows - NA_WIN_H)
        kabs = ws + kr
        row_ok = (kabs >= r0) & (kabs < r0 + NA_WIN_H)
        dr = kabs - r + NA_WIN_H - 1
        row_onehot[gi] = (dr[..., None] == np.arange(n_dr)) & row_ok[..., None]
        ok[gi] = row_ok[:, None, :, None] & col_ok[None, :, None, :]
    by_col = jnp.einsum("lhrd,dqc->lhrqc", rpb.astype(F32), col_onehot, precision=lax.Precision.HIGHEST)
    bias = jnp.einsum("gakr,lhrqc->lghaqkc", row_onehot, by_col, precision=lax.Precision.HIGHEST)
    bias = jnp.where(ok[None, :, None], bias * LOG2E, NEG_INF)
    return bias.reshape(rpb.shape[0], 3, NA_HEADS, NA_QROWS * GRID_W, NA_KROWS * GRID_W)


def _na_kernel(bias_ref, q_ref, kl_ref, vl_ref, kc_ref, vc_ref, o_ref):
    i = pl.program_id(1)
    n_steps = pl.num_programs(1)
    last_ws = 2048 // GRID_W - NA_KROWS
    ws = pl.multiple_of(jnp.clip(NA_QROWS * i - NA_WIN_H // 2, 0, last_ws) * GRID_W, NA_QROWS * GRID_W)
    geom = jnp.where(i == 0, 0, jnp.where(i == n_steps - 1, 2, 1))
    nkw = NA_KROWS * GRID_W
    outs = []
    for h in range(NA_HEADS):
        hs = slice(h * NA_DH, (h + 1) * NA_DH)
        vs = slice(h * 128, (h + 1) * 128)
        q = q_ref[:, hs]
        s_w = _dot_nt(q, kl_ref[pl.ds(ws, nkw), hs]) + bias_ref[geom, h]
        s_c = _dot_nt(q, kc_ref[:, hs])
        outs.append(_softmax_pv_scores([s_w, s_c], [vl_ref[pl.ds(ws, nkw), vs], vc_ref[:, vs]]))
    o_ref[...] = jnp.concatenate(outs, axis=-1).astype(BF16)


def _neigh_attention(bias, slab, slab_c, *, n, n_ctx):
    batch = slab.shape[0] // n
    tq = NA_QROWS * GRID_W
    steps = n // tq
    return pl.pallas_call(
        _na_kernel,
        grid=(batch, steps),
        in_specs=[
            _resident(bias.shape),
            pl.BlockSpec((tq, BR), lambda b, i: (b * steps + i, COL_QB)),
            pl.BlockSpec((n, BR), lambda b, i: (b, COL_KB)),
            pl.BlockSpec((n, 2 * BR), lambda b, i: (b, COL_VB)),
            pl.BlockSpec((n_ctx, BR), lambda b, i: (b, COL_KB)),
            pl.BlockSpec((n_ctx, 2 * BR), lambda b, i: (b, COL_VB)),
        ],
        out_specs=pl.BlockSpec((tq, BR), lambda b, i: (b * steps + i, 0)),
        out_shape=jax.ShapeDtypeStruct((batch * n, BR), BF16),
        compiler_params=_cparams(2),
        name="neigh_attention",
    )(bias, slab, slab, slab, slab_c, slab_c)


def _ctx_attn_kernel(q_ref, k_ref, v_ref, o_ref):
    outs = []
    for h in range(NA_HEADS):
        hs = slice(h * NA_DH, (h + 1) * NA_DH)
        outs.append(_softmax_pv(q_ref[:, hs], [(k_ref[:, hs], v_ref[:, h * 128:(h + 1) * 128])]))
    o_ref[...] = jnp.concatenate(outs, axis=-1).astype(BF16)


def _ctx_attention(slab_c, *, n_ctx):
    batch = slab_c.shape[0] // n_ctx
    return pl.pallas_call(
        _ctx_attn_kernel,
        grid=(batch,),
        in_specs=[
            pl.BlockSpec((n_ctx, BR), lambda b: (b, COL_QB)),
            pl.BlockSpec((n_ctx, BR), lambda b: (b, COL_KB)),
            pl.BlockSpec((n_ctx, 2 * BR), lambda b: (b, COL_VB)),
        ],
        out_specs=pl.BlockSpec((n_ctx, BR), lambda b: (b, 0)),
        out_shape=jax.ShapeDtypeStruct((batch * n_ctx, BR), BF16),
        compiler_params=_cparams(1),
        name="ctx_attention",
    )(slab_c, slab_c, slab_c)


HG_PAIR = 2 * HG_D
HG_LEVELS = (32, 16, 8, 4, 2, 1)
HG_UNROLL = 4


def _hgrn_tables():
    c = HG_CHUNK
    t = np.arange(c)[:, None]
    s = np.arange(HG_PAIR)[None, :] % c
    cum = np.stack([np.arange(c)[None, :] <= t, np.arange(c)[None, :] >= t]).astype(np.float32)
    side = np.zeros((2, len(HG_LEVELS), c, HG_PAIR), np.float32)
    out = np.zeros((2, len(HG_LEVELS) + 1, c, HG_PAIR), np.float32)
    for d in range(2):
        out[d, 0] = t == s
        for li, m in enumerate(HG_LEVELS):
            q_t = (t % (2 * m) >= m) if d == 0 else (t % (2 * m) < m)
            k_s = (s % (2 * m) < m) if d == 0 else (s % (2 * m) >= m)
            side[d, li] = np.broadcast_to(q_t, (c, HG_PAIR))
            out[d, li + 1] = (t // (2 * m) == s // (2 * m)) & q_t & k_s
    pos = np.broadcast_to(t % 4, (c, HG_PAIR))
    coef = np.zeros((2, 3, c, HG_PAIR), np.float32)
    coef[0, 0] = pos >= 2
    coef[1, 0] = pos < 2
    coef[:, 1] = pos == 0
    coef[:, 2] = pos == 3
    r = np.arange(HG_PAIR)
    bd = (r[:, None] // HG_D == r[None, :] // HG_D).astype(np.float32)
    return (jnp.asarray(cum, BF16), jnp.asarray(side), jnp.asarray(out), jnp.asarray(coef),
            jnp.asarray(bd, BF16), jnp.asarray(bd))


def _row_block_broadcast(b, size, offset):
    parts = [jnp.broadcast_to(b[s + offset:s + offset + 1, :], (size, b.shape[1])) for s in range(0, b.shape[0], size)]
    return parts[0] if len(parts) == 1 else jnp.concatenate(parts, axis=0)


def _hgrn_unit(q16, v16, g, d, cum_ref, side_ref, out_ref, coef_ref, bd16_ref, bd_ref):
    c = HG_CHUNK
    bd16 = bd16_ref[...]

    def block_diag(x16):
        return jnp.concatenate([x16, x16], axis=0) * bd16

    q = q16.astype(F32)
    g_hi = g.astype(BF16)
    g_lo = (g - g_hi.astype(F32)).astype(BF16)
    cum = cum_ref[d]
    b = _dot(cum, g_hi) + _dot(cum, g_lo)
    f = jnp.exp(g)
    k = 1.0 - f
    yield
    btot = b[0:1, :] if d == 1 else b[c - 1:c, :]

    def level_scores(li, x):
        x16 = x.astype(BF16)
        return out_ref[d, li + 1] * _dot_nt(x16, block_diag(x16))

    att = out_ref[d, 0] * _dot_nt(q16, block_diag(k.astype(BF16)))
    for li, m in enumerate(HG_LEVELS):
        is_q = side_ref[d, li] != 0.0
        if m >= 4:
            ref_b = _row_block_broadcast(b, 2 * m, m if d == 1 else m - 1)
            decay = jnp.exp(-jnp.abs(b - ref_b))
            x = jnp.where(is_q, q, k) * decay
        elif m == 2:
            e = coef_ref[d, 0] * g + coef_ref[d, 1] * pltpu.roll(g, c - 1, 0) + coef_ref[d, 2] * pltpu.roll(g, 1, 0)
            x = jnp.where(is_q, q, k) * jnp.exp(e)
        else:
            x = jnp.where(is_q, q * f, k)
        att = att + level_scores(li, x)
        yield

    o_intra = _dot(att.astype(BF16), block_diag(v16))
    q_in = (q * jnp.exp(b)).astype(BF16)
    yield
    upd = bd_ref[...] * _dot_tn(v16, (k * jnp.exp(btot - b)).astype(BF16))
    return o_intra, q_in, jnp.exp(btot), upd


def _hgrn_carry(st, o_intra, q_in, decay, upd):
    return o_intra + _dot_nt(q_in, st.astype(BF16)), st * decay + upd


def _interleave(units):
    results = [None] * len(units)
    live = list(range(len(units)))
    while live:
        for i in list(live):
            try:
                next(units[i])
            except StopIteration as done:
                results[i] = done.value
                live.remove(i)
    return results


def _hgrn_kernel(gain_ref, cum_ref, side_ref, out_ref, coef_ref, bd16_ref, bd_ref,
                 ql_ref, il_ref, gl_ref, qc_ref, ic_ref, gc_ref, yl_ref, yc_ref, st_ref, ol_ref, oc_ref):
    c = HG_CHUNK
    st_ref[...] = jnp.zeros(st_ref.shape, F32)

    def scan(q_ref, i_ref, g_ref, o_ref):
        n_chunks = q_ref.shape[0] // c
        pairs = [(d, p) for d in range(2) for p in range(BR // HG_PAIR)]

        def body(step, carry):
            where, units = [], []
            for u in range(HG_UNROLL):
                for d, p in pairs:
                    idx = step * HG_UNROLL + u
                    rows = pl.ds(pl.multiple_of((idx if d == 0 else n_chunks - 1 - idx) * c, c), c)
                    lanes = slice(p * HG_PAIR, (p + 1) * HG_PAIR)
                    g_lanes = slice(d * BR + p * HG_PAIR, d * BR + (p + 1) * HG_PAIR)
                    where.append((d, rows, lanes))
                    units.append(_hgrn_unit(q_ref[rows, lanes], i_ref[rows, lanes], g_ref[rows, g_lanes],
                                            d, cum_ref, side_ref, out_ref, coef_ref, bd16_ref, bd_ref))
            parts = _interleave(units)
            states = [st_ref[d, p] for d, p in pairs]
            for j, ((d, rows, lanes), part) in enumerate(zip(where, parts)):
                o, states[j % len(pairs)] = _hgrn_carry(states[j % len(pairs)], *part)
                o_ref[d, rows, lanes] = o
            for (d, p), st in zip(pairs, states):
                st_ref[d, p] = st
            return carry

        lax.fori_loop(0, n_chunks // HG_UNROLL, body, 0)

    scan(qc_ref, ic_ref, gc_ref, oc_ref)
    scan(ql_ref, il_ref, gl_ref, ol_ref)
    yl_ref[...] = _group_rmsnorm(ol_ref[0] + ol_ref[1], gain_ref[...], HG_D).astype(BF16)
    yc_ref[...] = _group_rmsnorm(oc_ref[0] + oc_ref[1], gain_ref[...], HG_D).astype(BF16)


def _hgrn(gain, tables, slab, g, slab_c, g_c, *, n, n_ctx):
    batch = slab.shape[0] // n
    return pl.pallas_call(
        _hgrn_kernel,
        grid=(batch,),
        in_specs=[
            _resident((1, BR)),
            *[_resident(t.shape) for t in tables],
            pl.BlockSpec((n, BR), lambda b: (b, COL_HQ)),
            pl.BlockSpec((n, BR), lambda b: (b, COL_HI)),
            pl.BlockSpec((n, 2 * BR), lambda b: (b, 0)),
            pl.BlockSpec((n_ctx, BR), lambda b: (b, COL_HQ)),
            pl.BlockSpec((n_ctx, BR), lambda b: (b, COL_HI)),
            pl.BlockSpec((n_ctx, 2 * BR), lambda b: (b, 0)),
        ],
        out_specs=[
            pl.BlockSpec((n, BR), lambda b: (b, 0)),
            pl.BlockSpec((n_ctx, BR), lambda b: (b, 0)),
        ],
        out_shape=[
            jax.ShapeDtypeStruct((batch * n, BR), BF16),
            jax.ShapeDtypeStruct((batch * n_ctx, BR), BF16),
        ],
        scratch_shapes=[
            pltpu.VMEM((2, BR // HG_PAIR, HG_PAIR, HG_PAIR), F32),
            pltpu.VMEM((2, n, BR), F32),
            pltpu.VMEM((2, n_ctx, BR), F32),
        ],
        compiler_params=_cparams(1),
        name="hgrn2",
    )(gain, *tables, slab, slab, g, slab_c, slab_c, g_c)


def _dft_tables(n):
    lo = 64
    hi = n // lo
    kk = np.arange(n, dtype=np.int64)
    ang_hi = 2.0 * np.pi * ((np.arange(hi, dtype=np.int64)[:, None] * lo * kk[None, :]) % n) / n
    ang_lo = 2.0 * np.pi * ((np.arange(lo, dtype=np.int64)[:, None] * kk[None, :]) % n) / n
    ch, sh = (jnp.asarray(f(ang_hi), F32)[:, None, :] for f in (np.cos, np.sin))
    cl, sl = (jnp.asarray(f(ang_lo), F32)[None, :, :] for f in (np.cos, np.sin))
    scale = 1.0 / math.sqrt(n)
    cos = ((ch * cl - sh * sl) * scale).reshape(n, n)
    sin = ((sh * cl + ch * sl) * scale).reshape(n, n)
    return jnp.concatenate([cos, -sin], axis=1).astype(BF16)


def _channel_dft_tables():
    j = np.arange(BR)
    same = (j[:, None] // FT_DG) == (j[None, :] // FT_DG)
    ang = 2.0 * np.pi * (((j[:, None] % FT_DG) * (j[None, :] % FT_DG)) % FT_DG) / FT_DG
    scale = 1.0 / math.sqrt(FT_DG)
    return (jnp.asarray(np.where(same, np.cos(ang), 0.0) * scale, BF16),
            jnp.asarray(np.where(same, np.sin(ang), 0.0) * scale, BF16))


def _fourier_kernel(cs_ref, bc_ref, bs_ref, u_ref, o_ref):
    u = u_ref[...]
    stacked = jnp.concatenate([_dot(u, bc_ref[...]).astype(BF16), _dot(u, bs_ref[...]).astype(BF16)], axis=0)
    o_ref[...] = _dot(cs_ref[...], stacked).astype(BF16)


def _fourier(cs, bc, bs, slab, *, n):
    batch = slab.shape[0] // n
    return pl.pallas_call(
        _fourier_kernel,
        grid=(batch,),
        in_specs=[
            _resident((n, 2 * n)),
            _resident((BR, BR)),
            _resident((BR, BR)),
            pl.BlockSpec((n, BR), lambda b: (b, COL_U)),
        ],
        out_specs=pl.BlockSpec((n, BR), lambda b: (b, 0)),
        out_shape=jax.ShapeDtypeStruct((batch * n, BR), BF16),
        compiler_params=_cparams(1),
        name="fourier",
    )(cs, bc, bs, slab)


def _merge_kernel(x_ref, mod_ref, ng_ref, ya_ref, yb_ref, yh_ref, yf_ref, sg_ref, wup_ref, wmg_ref, wout_ref, o_ref):
    x = x_ref[...]
    h = _modulated_norm(x, mod_ref, ng_ref).astype(BF16)
    acc = None
    for i, y_ref in enumerate((ya_ref, yb_ref, yh_ref, yf_ref)):
        y = (y_ref[...].astype(F32) * sg_ref[:, i * BR:(i + 1) * BR].astype(F32)).astype(BF16)
        term = jax.nn.sigmoid(_dot(h, wmg_ref[i])) * _dot(y, wup_ref[i])
        acc = term if acc is None else acc + term
    o_ref[...] = x + mod_ref[2:3, :] * _dot(acc.astype(BF16), wout_ref[...])


def _merge(x2d, mod_l, ng, ys, slab, wup, wmg, wout, *, tm, rows_per_mod, mod_row0):
    n = x2d.shape[0]

    def mod_map(t):
        return ((t * tm) // rows_per_mod + mod_row0, 0, 0)

    return pl.pallas_call(
        _merge_kernel,
        grid=(n // tm,),
        in_specs=[
            pl.BlockSpec((tm, D_MODEL), lambda t: (t, 0)),
            pl.BlockSpec((None, 3, D_MODEL), mod_map),
            _resident((1, D_MODEL)),
            *[pl.BlockSpec((tm, BR), lambda t: (t, 0)) for _ in range(4)],
            pl.BlockSpec((tm, 4 * BR), lambda t: (t, COL_SG)),
            _resident((4, BR, D_MODEL)),
            _resident((4, D_MODEL, D_MODEL)),
            _resident((D_MODEL, D_MODEL)),
        ],
        out_specs=pl.BlockSpec((tm, D_MODEL), lambda t: (t, 0)),
        out_shape=jax.ShapeDtypeStruct((n, D_MODEL), F32),
        compiler_params=_cparams(1),
        name="merge",
    )(x2d, mod_l, ng, *ys, slab, wup, wmg, wout)


def _rope_tables(n):
    t = jnp.arange(n)
    row = (t // GRID_W).astype(F32)
    col = (t % GRID_W).astype(F32)
    d_axis = DA_DH // 2
    inv = ROPE_THETA ** (-jnp.arange(0, d_axis, 2, dtype=F32) / d_axis)
    ang = jnp.concatenate([row[:, None] * inv, col[:, None] * inv], axis=-1)
    cos, sin = jnp.cos(ang), jnp.sin(ang)
    cos_h = jnp.concatenate([cos, cos], axis=-1)
    sin_h = jnp.concatenate([-sin, sin], axis=-1)
    return jnp.tile(cos_h, (1, 2 * DA_HEADS)), jnp.tile(sin_h, (1, 2 * DA_HEADS))


def kernel(x, c, ctx, c_ctx, norm_gain, w_mod, b_mod, w_in, da_qk_gain, da_lambda, da_subln_gain,
           na_qk_gain, na_rpb, hg_lb_logits, hg_norm_gain, w_up, w_merge, w_out):
    batch, n, _ = x.shape
    n_ctx = ctx.shape[1]
    assert n == 2048 and n % GRID_W == 0 and batch + 1 <= MOD_ROWS

    cc = jnp.concatenate([c, c_ctx[None, :], jnp.zeros((MOD_ROWS - batch - 1, D_MODEL), F32)], axis=0)
    mod = _modulation(cc, w_mod, b_mod).reshape(DEPTH, MOD_ROWS, 3, D_MODEL)

    p_lb = jax.nn.softmax(hg_lb_logits.astype(F32), axis=1)
    lower = jnp.cumsum(p_lb, axis=1) - p_lb[:, :1]
    lb_floored = jnp.maximum(lower, LB_FLOOR)
    one_minus_lb = 1.0 - lower
    lv = da_lambda.astype(F32)
    lam_all = jnp.exp(jnp.sum(lv[:, 0] * lv[:, 1], axis=-1)) - jnp.exp(jnp.sum(lv[:, 2] * lv[:, 3], axis=-1))

    cos, sin = _rope_tables(n)
    cs_lat, cs_ctx = _dft_tables(n), _dft_tables(n_ctx)
    bc, bs = _channel_dft_tables()
    bias_all = _na_bias_tables(na_rpb)
    hg_tables = _hgrn_tables()

    xl = x.reshape(batch * n, D_MODEL)
    xc = ctx.reshape(batch * n_ctx, D_MODEL)
    for l in range(DEPTH):
        need_ctx = l < DEPTH - 1
        lam_init = 0.8 - 0.6 * math.exp(-0.3 * l)
        lam = (lam_all[l] + lam_init).reshape(1)
        ng = norm_gain[l].reshape(1, D_MODEL)
        w = w_in[l].astype(BF16)
        gda = jnp.tile(da_qk_gain[l], (1, BR // DA_DH)) * jnp.array([[DA_DH ** -0.5 * LOG2E], [1.0]], F32)
        gna = jnp.tile(na_qk_gain[l], (1, BR // NA_DH)) * jnp.array([[NA_DH ** -0.5 * LOG2E], [1.0]], F32)
        lbp = jnp.stack([lb_floored[0, l], one_minus_lb[0, l], lb_floored[1, l], one_minus_lb[1, l]])
        subln = da_subln_gain[l].reshape(1, DA_DV)
        hgain = jnp.tile(hg_norm_gain[l].reshape(1, HG_D), (1, HG_HEADS))
        bias = bias_all[l]

        slab, g = _inproj(xl, mod[l], ng, w, gda, gna, lbp, cos, sin,
                          tm=512, rows_per_mod=n, mod_row0=0, rope=True)
        slab_c, g_c = _inproj(xc, mod[l], ng, w, gda, gna, lbp, cos[:n_ctx], sin[:n_ctx],
                              tm=n_ctx, rows_per_mod=batch * n_ctx, mod_row0=batch, rope=False)

        ya = _diff_attention(lam, subln, slab, [(slab, n), (slab_c, n_ctx)], nq=n, tq=512, lam_init=lam_init)
        yb = _neigh_attention(bias, slab, slab_c, n=n, n_ctx=n_ctx)
        yh, yh_c = _hgrn(hgain, hg_tables, slab, g, slab_c, g_c, n=n, n_ctx=n_ctx)
        yf = _fourier(cs_lat, bc, bs, slab, n=n)

        wup = w_up[l].astype(BF16)
        wmg = w_merge[l].astype(BF16)
        wout = w_out[l].astype(BF16)
        xl = _merge(xl, mod[l], ng, (ya, yb, yh, yf), slab, wup, wmg, wout,
                    tm=512, rows_per_mod=n, mod_row0=0)
        if need_ctx:
            ya_c = _diff_attention(lam, subln, slab_c, [(slab_c, n_ctx)], nq=n_ctx, tq=n_ctx, lam_init=lam_init)
            yb_c = _ctx_attention(slab_c, n_ctx=n_ctx)
            yf_c = _fourier(cs_ctx, bc, bs, slab_c, n=n_ctx)
            xc = _merge(xc, mod[l], ng, (ya_c, yb_c, yh_c, yf_c), slab_c, wup, wmg, wout,
                        tm=n_ctx, rows_per_mod=batch * n_ctx, mod_row0=batch)
    return xl.reshape(batch, n, D_MODEL)
```

```python
import functools
import math

import numpy as np
import jax
import jax.numpy as jnp
from jax import lax
from jax.experimental import pallas as pl
from jax.experimental.pallas import tpu as pltpu

D_MODEL = 1024
DEPTH = 4
GRID_W = 64
BR = 256
DA_HEADS, DA_DH, DA_DV = 4, 32, 64
NA_HEADS, NA_DH, NA_WIN_H, NA_WIN_W = 4, 64, 8, 16
HG_HEADS, HG_D, HG_CHUNK = 4, 64, 64
FT_DG = 64
IN_WIDTH = 15 * BR
LB_FLOOR = 1e-20
ROPE_THETA = 10000.0
EPS = 1e-6
NEG_INF = -1e30

COL_SG = 0
COL_QA, COL_KA, COL_QB, COL_KB = 4, 5, 6, 7
COL_VA, COL_VB = 4, 5
COL_HQ, COL_HI, COL_U = 12, 13, 14
SLAB_W = 15 * BR
LOG2E = 1.4426950408889634

DA_QUNIT = 512
NA_QROWS = 4
NA_KROWS = 12
MOD_ROWS = 24

VMEM_LIMIT_BYTES = 56 * 1024 * 1024

F32 = jnp.float32
BF16 = jnp.bfloat16


def _cparams(n_axes):
    return pltpu.CompilerParams(dimension_semantics=("arbitrary",) * n_axes, vmem_limit_bytes=VMEM_LIMIT_BYTES)


def _resident(shape):
    nd = len(shape)
    return pl.BlockSpec(shape, lambda *_: (0,) * nd, pipeline_mode=pl.Buffered(1))


def _dot(a, b):
    return jnp.dot(a, b, preferred_element_type=F32)


def _dot_nt(a, b):
    return lax.dot_general(a, b, (((1,), (1,)), ((), ())), preferred_element_type=F32)


def _dot_tn(a, b):
    return lax.dot_general(a, b, (((0,), (0,)), ((), ())), preferred_element_type=F32)


def _split3(v):
    hi = v.astype(BF16)
    r = v - hi.astype(F32)
    mid = r.astype(BF16)
    lo = (r - mid.astype(F32)).astype(BF16)
    return hi, mid, lo


def _group_mean(v, gsize, passes):
    w = v.shape[-1]
    r = lax.broadcasted_iota(jnp.int32, (w, w), 0) // gsize
    c = lax.broadcasted_iota(jnp.int32, (w, w), 1) // gsize
    ones = jnp.where(r == c, 1.0 / gsize, 0.0).astype(BF16)
    total, rest = None, v
    for _ in range(passes):
        piece = rest.astype(BF16)
        rest = rest - piece.astype(F32)
        part = _dot(piece, ones)
        total = part if total is None else total + part
    return total


def _group_rmsnorm(v, gain, gsize, passes=2):
    return v * lax.rsqrt(_group_mean(v * v, gsize, passes) + EPS) * gain


def _silu(v):
    return v * jax.nn.sigmoid(v)


def _mod_kernel(c_ref, w_ref, b_ref, o_ref):
    s = _silu(c_ref[...]).astype(BF16)
    o_ref[...] = _dot(s, w_ref[...].astype(BF16)) + b_ref[...]


def _modulation(cc, w_mod, b_mod):
    tn = D_MODEL
    return pl.pallas_call(
        _mod_kernel,
        grid=(DEPTH, 3 * D_MODEL // tn),
        in_specs=[
            pl.BlockSpec((MOD_ROWS, D_MODEL), lambda l, j: (0, 0)),
            pl.BlockSpec((None, D_MODEL, tn), lambda l, j: (l, 0, j)),
            pl.BlockSpec((None, 1, tn), lambda l, j: (l, 0, j)),
        ],
        out_specs=pl.BlockSpec((None, MOD_ROWS, tn), lambda l, j: (l, 0, j)),
        out_shape=jax.ShapeDtypeStruct((DEPTH, MOD_ROWS, 3 * D_MODEL), F32),
        compiler_params=_cparams(2),
        name="modulation",
    )(cc, w_mod, b_mod.reshape(DEPTH, 1, 3 * D_MODEL))


def _modulated_norm(x, mod_ref, ng_ref):
    ms = jnp.mean(x * x, axis=-1, keepdims=True)
    row_scale = ng_ref[...] * (1.0 + mod_ref[1:2, :])
    return x * lax.rsqrt(ms + EPS) * row_scale + mod_ref[0:1, :]


def _log_forget(f, lb_floored, one_minus_lb):
    return jnp.log(lb_floored + one_minus_lb * jax.nn.sigmoid(f))


def _rope(v, cos, sin_signed):
    lane = lax.broadcasted_iota(jnp.int32, v.shape, 1)
    first_half = (lane % DA_DH) < (DA_DH // 2)
    swapped = jnp.where(first_half, pltpu.roll(v, BR - DA_DH // 2, 1), pltpu.roll(v, DA_DH // 2, 1))
    return v * cos + swapped * sin_signed


def _inproj_kernel(x_ref, mod_ref, ng_ref, w_ref, gda_ref, gna_ref, lb_ref, cos_ref, sin_ref, slab_ref, g_ref, *, rope):
    h = _modulated_norm(x_ref[...], mod_ref, ng_ref).astype(BF16)

    def proj(col):
        return _dot(h, w_ref[:, col * BR:(col + 1) * BR])

    def put(col, v):
        slab_ref[:, col * BR:(col + 1) * BR] = v.astype(BF16)

    def put_values(col2, v):
        ones = jnp.ones((v.shape[0], 64), F32)
        pieces = []
        for hd in range(BR // 64):
            pieces += [v[:, hd * 64:(hd + 1) * 64], ones]
        slab_ref[:, col2 * 2 * BR:(col2 + 1) * 2 * BR] = jnp.concatenate(pieces, axis=-1).astype(BF16)

    g_ref[:, 0:BR] = _log_forget(proj(7), lb_ref[0:1, :], lb_ref[1:2, :])
    g_ref[:, BR:2 * BR] = _log_forget(proj(8), lb_ref[2:3, :], lb_ref[3:4, :])
    q = _group_rmsnorm(proj(0), gda_ref[0:1, :], DA_DH, passes=1)
    k = _group_rmsnorm(proj(1), gda_ref[1:2, :], DA_DH, passes=1)
    if rope:
        q = _rope(q, cos_ref[...], sin_ref[...])
        k = _rope(k, cos_ref[...], sin_ref[...])
    put(COL_QA, q)
    put(COL_KA, k)
    put(COL_QB, _group_rmsnorm(proj(3), gna_ref[0:1, :], NA_DH, passes=1))
    put(COL_KB, _group_rmsnorm(proj(4), gna_ref[1:2, :], NA_DH, passes=1))
    for i in range(4):
        put(COL_SG + i, _silu(proj(11 + i)))
    put(COL_HQ, _silu(proj(6)))
    put_values(COL_VA, proj(2))
    put_values(COL_VB, proj(5))
    put(COL_HI, proj(9))
    put(COL_U, proj(10))


def _inproj(x2d, mod_l, ng, w, gda, gna, lbp, cos, sin, *, tm, rows_per_mod, mod_row0, rope):
    n = x2d.shape[0]
    n_pos = cos.shape[0]
    tiles_per_seq = n_pos // tm

    def mod_map(t):
        return ((t * tm) // rows_per_mod + mod_row0, 0, 0)

    return pl.pallas_call(
        functools.partial(_inproj_kernel, rope=rope),
        grid=(n // tm,),
        in_specs=[
            pl.BlockSpec((tm, D_MODEL), lambda t: (t, 0)),
            pl.BlockSpec((None, 3, D_MODEL), mod_map),
            _resident((1, D_MODEL)),
            _resident((D_MODEL, IN_WIDTH)),
            _resident((2, BR)),
            _resident((2, BR)),
            _resident((4, BR)),
            pl.BlockSpec((tm, BR), lambda t: (t % tiles_per_seq, 0)),
            pl.BlockSpec((tm, BR), lambda t: (t % tiles_per_seq, 0)),
        ],
        out_specs=[
            pl.BlockSpec((tm, SLAB_W), lambda t: (t, 0)),
            pl.BlockSpec((tm, 2 * BR), lambda t: (t, 0)),
        ],
        out_shape=[
            jax.ShapeDtypeStruct((n, SLAB_W), BF16),
            jax.ShapeDtypeStruct((n, 2 * BR), F32),
        ],
        compiler_params=_cparams(1),
        name="inproj_rope" if rope else "inproj",
    )(x2d, mod_l, ng, w, gda, gna, lbp, cos, sin)


def _softmax_pv_scores(scores, values):
    m = scores[0].max(axis=-1, keepdims=True)
    for s in scores[1:]:
        m = jnp.maximum(m, s.max(axis=-1, keepdims=True))
    acc = None
    for s, v in zip(scores, values):
        p = _dot(jnp.exp2(s - m).astype(BF16), v)
        acc = p if acc is None else acc + p
    return (acc / pltpu.roll(acc, 64, 1))[:, :64]


def _softmax_pv(q, segments):
    return _softmax_pv_scores([_dot_nt(q, k) for k, _ in segments], [v for _, v in segments])


def _da_kernel(lam_ref, g_ref, q_ref, *refs, lam_init, n_seg):
    kv_refs, o_ref = refs[:2 * n_seg], refs[2 * n_seg]
    lam = lam_ref[0]
    for r0 in range(0, q_ref.shape[0], DA_QUNIT):
        rows = slice(r0, min(r0 + DA_QUNIT, q_ref.shape[0]))
        outs = []
        for h in range(DA_HEADS):
            vs = slice(h * 128, (h + 1) * 128)
            att = []
            for j in range(2):
                qs = slice((2 * h + j) * DA_DH, (2 * h + j + 1) * DA_DH)
                segs = [(kv_refs[2 * s][:, qs], kv_refs[2 * s + 1][:, vs]) for s in range(n_seg)]
                att.append(_softmax_pv(q_ref[rows, qs], segs))
            o = att[0] - lam * att[1]
            o = o * lax.rsqrt(jnp.mean(o * o, axis=-1, keepdims=True) + EPS) * g_ref[...] * (1.0 - lam_init)
            outs.append(o)
        o_ref[rows, :] = jnp.concatenate(outs, axis=-1).astype(BF16)


def _diff_attention(lam, subln, slab_q, key_slabs, *, nq, tq, lam_init):
    batch = slab_q.shape[0] // nq
    qb = nq // tq
    in_specs = [
        pl.BlockSpec(memory_space=pltpu.SMEM),
        _resident((1, DA_DV)),
        pl.BlockSpec((tq, BR), lambda b, i: (b * qb + i, COL_QA)),
    ]
    args = [lam, subln, slab_q]
    for slab, nk in key_slabs:
        in_specs += [pl.BlockSpec((nk, BR), lambda b, i: (b, COL_KA)),
                     pl.BlockSpec((nk, 2 * BR), lambda b, i: (b, COL_VA))]
        args += [slab, slab]
    return pl.pallas_call(
        functools.partial(_da_kernel, lam_init=lam_init, n_seg=len(key_slabs)),
        grid=(batch, qb),
        in_specs=in_specs,
        out_specs=pl.BlockSpec((tq, BR), lambda b, i: (b * qb + i, 0)),
        out_shape=jax.ShapeDtypeStruct((batch * nq, BR), BF16),
        compiler_params=_cparams(2),
        name="diff_attention",
    )(*args)


def _na_bias_tables(rpb):
    rows = 2048 // GRID_W
    n_dr, n_dc = 2 * NA_WIN_H - 1, 2 * NA_WIN_W - 1
    geoms = [(0, 0), (NA_QROWS, 0), (rows - NA_QROWS, rows - NA_KROWS)]
    qc = np.arange(GRID_W)[:, None]
    kc = np.arange(GRID_W)[None, :]
    c0 = np.clip(qc - NA_WIN_W // 2, 0, GRID_W - NA_WIN_W)
    col_ok = (kc >= c0) & (kc < c0 + NA_WIN_W)
    dc = np.clip(kc - qc, 1 - NA_WIN_W, NA_WIN_W - 1) + NA_WIN_W - 1
    col_onehot = (dc[None] == np.arange(n_dc)[:, None, None]).astype(np.float32)
    qr = np.arange(NA_QROWS)[:, None]
    kr = np.arange(NA_KROWS)[None, :]
    row_onehot = np.zeros((3, NA_QROWS, NA_KROWS, n_dr), np.float32)
    ok = np.zeros((3, NA_QROWS, GRID_W, NA_KROWS, GRID_W), bool)
    for gi, (q0, ws) in enumerate(geoms):
        r = q0 + qr
        r0 = np.clip(r - NA_WIN_H // 2, 0, rows - NA_WIN_H)
        kabs = ws + kr
        row_ok = (kabs >= r0) & (kabs < r0 + NA_WIN_H)
        dr = kabs - r + NA_WIN_H - 1
        row_onehot[gi] = (dr[..., None] == np.arange(n_dr)) & row_ok[..., None]
        ok[gi] = row_ok[:, None, :, None] & col_ok[None, :, None, :]
    by_col = jnp.einsum("lhrd,dqc->lhrqc", rpb.astype(F32), col_onehot, precision=lax.Precision.HIGHEST)
    bias = jnp.einsum("gakr,lhrqc->lghaqkc", row_onehot, by_col, precision=lax.Precision.HIGHEST)
    bias = jnp.where(ok[None, :, None], bias * LOG2E, NEG_INF)
    return bias.reshape(rpb.shape[0], 3, NA_HEADS, NA_QROWS * GRID_W, NA_KROWS * GRID_W)


def _na_kernel(bias_ref, q_ref, kl_ref, vl_ref, kc_ref, vc_ref, o_ref):
    tq = NA_QROWS * GRID_W
    nkw = NA_KROWS * GRID_W
    n_units = q_ref.shape[0] // tq
    last_ws = q_ref.shape[0] // GRID_W - NA_KROWS
    for i in range(n_units):
        ws = min(max(NA_QROWS * i - NA_WIN_H // 2, 0), last_ws) * GRID_W
        geom = 0 if i == 0 else (2 if i == n_units - 1 else 1)
        rows = slice(i * tq, (i + 1) * tq)
        win = slice(ws, ws + nkw)
        outs = []
        for h in range(NA_HEADS):
            hs = slice(h * NA_DH, (h + 1) * NA_DH)
            vs = slice(h * 128, (h + 1) * 128)
            q = q_ref[rows, hs]
            s_w = _dot_nt(q, kl_ref[win, hs]) + bias_ref[geom, h]
            s_c = _dot_nt(q, kc_ref[:, hs])
            outs.append(_softmax_pv_scores([s_w, s_c], [vl_ref[win, vs], vc_ref[:, vs]]))
        o_ref[rows, :] = jnp.concatenate(outs, axis=-1).astype(BF16)


def _neigh_attention(bias_all, layer, slab, slab_c, *, n, n_ctx):
    batch = slab.shape[0] // n
    return pl.pallas_call(
        _na_kernel,
        grid=(batch,),
        in_specs=[
            pl.BlockSpec((None,) + bias_all.shape[1:], lambda b: (layer, 0, 0, 0, 0), pipeline_mode=pl.Buffered(1)),
            pl.BlockSpec((n, BR), lambda b: (b, COL_QB)),
            pl.BlockSpec((n, BR), lambda b: (b, COL_KB)),
            pl.BlockSpec((n, 2 * BR), lambda b: (b, COL_VB)),
            pl.BlockSpec((n_ctx, BR), lambda b: (b, COL_KB)),
            pl.BlockSpec((n_ctx, 2 * BR), lambda b: (b, COL_VB)),
        ],
        out_specs=pl.BlockSpec((n, BR), lambda b: (b, 0)),
        out_shape=jax.ShapeDtypeStruct((batch * n, BR), BF16),
        compiler_params=_cparams(1),
        name="neigh_attention",
    )(bias_all, slab, slab, slab, slab_c, slab_c)


def _ctx_attn_kernel(q_ref, k_ref, v_ref, o_ref):
    outs = []
    for h in range(NA_HEADS):
        hs = slice(h * NA_DH, (h + 1) * NA_DH)
        outs.append(_softmax_pv(q_ref[:, hs], [(k_ref[:, hs], v_ref[:, h * 128:(h + 1) * 128])]))
    o_ref[...] = jnp.concatenate(outs, axis=-1).astype(BF16)


def _ctx_attention(slab_c, *, n_ctx):
    batch = slab_c.shape[0] // n_ctx
    return pl.pallas_call(
        _ctx_attn_kernel,
        grid=(batch,),
        in_specs=[
            pl.BlockSpec((n_ctx, BR), lambda b: (b, COL_QB)),
            pl.BlockSpec((n_ctx, BR), lambda b: (b, COL_KB)),
            pl.BlockSpec((n_ctx, 2 * BR), lambda b: (b, COL_VB)),
        ],
        out_specs=pl.BlockSpec((n_ctx, BR), lambda b: (b, 0)),
        out_shape=jax.ShapeDtypeStruct((batch * n_ctx, BR), BF16),
        compiler_params=_cparams(1),
        name="ctx_attention",
    )(slab_c, slab_c, slab_c)


HG_PAIR = 2 * HG_D
HG_LEVELS = (32, 16, 8, 4, 2, 1)
HG_UNROLL = 4


def _hgrn_tables():
    c = HG_CHUNK
    t = np.arange(c)[:, None]
    s = np.arange(HG_PAIR)[None, :] % c
    cum = np.stack([np.arange(c)[None, :] <= t, np.arange(c)[None, :] >= t]).astype(np.float32)
    side = np.zeros((2, len(HG_LEVELS), c, HG_PAIR), np.float32)
    out = np.zeros((2, len(HG_LEVELS) + 1, c, HG_PAIR), np.float32)
    for d in range(2):
        out[d, 0] = t == s
        for li, m in enumerate(HG_LEVELS):
            q_t = (t % (2 * m) >= m) if d == 0 else (t % (2 * m) < m)
            k_s = (s % (2 * m) < m) if d == 0 else (s % (2 * m) >= m)
            side[d, li] = np.broadcast_to(q_t, (c, HG_PAIR))
            out[d, li + 1] = (t // (2 * m) == s // (2 * m)) & q_t & k_s
    pos = np.broadcast_to(t % 4, (c, HG_PAIR))
    coef = np.zeros((2, 3, c, HG_PAIR), np.float32)
    coef[0, 0] = pos >= 2
    coef[1, 0] = pos < 2
    coef[:, 1] = pos == 0
    coef[:, 2] = pos == 3
    r = np.arange(HG_PAIR)
    bd = (r[:, None] // HG_D == r[None, :] // HG_D).astype(np.float32)
    return (jnp.asarray(cum, BF16), jnp.asarray(side), jnp.asarray(out), jnp.asarray(coef),
            jnp.asarray(bd, BF16), jnp.asarray(bd))


def _row_block_broadcast(b, size, offset):
    parts = [jnp.broadcast_to(b[s + offset:s + offset + 1, :], (size, b.shape[1])) for s in range(0, b.shape[0], size)]
    return parts[0] if len(parts) == 1 else jnp.concatenate(parts, axis=0)


def _hgrn_unit(q16, v16, g, d, cum_ref, side_ref, out_ref, coef_ref, bd16_ref, bd_ref):
    c = HG_CHUNK
    bd16 = bd16_ref[...]

    def block_diag(x16):
        return jnp.concatenate([x16, x16], axis=0) * bd16

    q = q16.astype(F32)
    g_hi = g.astype(BF16)
    g_lo = (g - g_hi.astype(F32)).astype(BF16)
    cum = cum_ref[d]
    b = _dot(cum, g_hi) + _dot(cum, g_lo)
    f = jnp.exp(g)
    k = 1.0 - f
    yield
    btot = b[0:1, :] if d == 1 else b[c - 1:c, :]

    def level_scores(li, x):
        x16 = x.astype(BF16)
        return out_ref[d, li + 1] * _dot_nt(x16, block_diag(x16))

    att = out_ref[d, 0] * _dot_nt(q16, block_diag(k.astype(BF16)))
    for li, m in enumerate(HG_LEVELS):
        is_q = side_ref[d, li] != 0.0
        if m >= 4:
            ref_b = _row_block_broadcast(b, 2 * m, m if d == 1 else m - 1)
            decay = jnp.exp(-jnp.abs(b - ref_b))
            x = jnp.where(is_q, q, k) * decay
        elif m == 2:
            e = coef_ref[d, 0] * g + coef_ref[d, 1] * pltpu.roll(g, c - 1, 0) + coef_ref[d, 2] * pltpu.roll(g, 1, 0)
            x = jnp.where(is_q, q, k) * jnp.exp(e)
        else:
            x = jnp.where(is_q, q * f, k)
        att = att + level_scores(li, x)
        yield

    o_intra = _dot(att.astype(BF16), block_diag(v16))
    q_in = (q * jnp.exp(b)).astype(BF16)
    yield
    upd = bd_ref[...] * _dot_tn(v16, (k * jnp.exp(btot - b)).astype(BF16))
    return o_intra, q_in, jnp.exp(btot), upd


def _hgrn_carry(st, o_intra, q_in, decay, upd):
    return o_intra + _dot_nt(q_in, st.astype(BF16)), st * decay + upd


def _interleave(units):
    results = [None] * len(units)
    live = list(range(len(units)))
    while live:
        for i in list(live):
            try:
                next(units[i])
            except StopIteration as done:
                results[i] = done.value
                live.remove(i)
    return results


def _hgrn_kernel(gain_ref, cum_ref, side_ref, out_ref, coef_ref, bd16_ref, bd_ref,
                 ql_ref, il_ref, gl_ref, qc_ref, ic_ref, gc_ref, yl_ref, yc_ref, st_ref, ol_ref, oc_ref):
    c = HG_CHUNK
    st_ref[...] = jnp.zeros(st_ref.shape, F32)

    def scan(q_ref, i_ref, g_ref, o_ref):
        n_chunks = q_ref.shape[0] // c
        pairs = [(d, p) for d in range(2) for p in range(BR // HG_PAIR)]

        def body(step, carry):
            where, units = [], []
            for u in range(HG_UNROLL):
                for d, p in pairs:
                    idx = step * HG_UNROLL + u
                    rows = pl.ds(pl.multiple_of((idx if d == 0 else n_chunks - 1 - idx) * c, c), c)
                    lanes = slice(p * HG_PAIR, (p + 1) * HG_PAIR)
                    g_lanes = slice(d * BR + p * HG_PAIR, d * BR + (p + 1) * HG_PAIR)
                    where.append((d, rows, lanes))
                    units.append(_hgrn_unit(q_ref[rows, lanes], i_ref[rows, lanes], g_ref[rows, g_lanes],
                                            d, cum_ref, side_ref, out_ref, coef_ref, bd16_ref, bd_ref))
            parts = _interleave(units)
            states = [st_ref[d, p] for d, p in pairs]
            for j, ((d, rows, lanes), part) in enumerate(zip(where, parts)):
                o, states[j % len(pairs)] = _hgrn_carry(states[j % len(pairs)], *part)
                o_ref[d, rows, lanes] = o
            for (d, p), st in zip(pairs, states):
                st_ref[d, p] = st
            return carry

        lax.fori_loop(0, n_chunks // HG_UNROLL, body, 0)

    scan(qc_ref, ic_ref, gc_ref, oc_ref)
    scan(ql_ref, il_ref, gl_ref, ol_ref)
    yl_ref[...] = _group_rmsnorm(ol_ref[0] + ol_ref[1], gain_ref[...], HG_D).astype(BF16)
    yc_ref[...] = _group_rmsnorm(oc_ref[0] + oc_ref[1], gain_ref[...], HG_D).astype(BF16)


def _hgrn(gain, tables, slab, g, slab_c, g_c, *, n, n_ctx):
    batch = slab.shape[0] // n
    return pl.pallas_call(
        _hgrn_kernel,
        grid=(batch,),
        in_specs=[
            _resident((1, BR)),
            *[_resident(t.shape) for t in tables],
            pl.BlockSpec((n, BR), lambda b: (b, COL_HQ)),
            pl.BlockSpec((n, BR), lambda b: (b, COL_HI)),
            pl.BlockSpec((n, 2 * BR), lambda b: (b, 0)),
            pl.BlockSpec((n_ctx, BR), lambda b: (b, COL_HQ)),
            pl.BlockSpec((n_ctx, BR), lambda b: (b, COL_HI)),
            pl.BlockSpec((n_ctx, 2 * BR), lambda b: (b, 0)),
        ],
        out_specs=[
            pl.BlockSpec((n, BR), lambda b: (b, 0)),
            pl.BlockSpec((n_ctx, BR), lambda b: (b, 0)),
        ],
        out_shape=[
            jax.ShapeDtypeStruct((batch * n, BR), BF16),
            jax.ShapeDtypeStruct((batch * n_ctx, BR), BF16),
        ],
        scratch_shapes=[
            pltpu.VMEM((2, BR // HG_PAIR, HG_PAIR, HG_PAIR), F32),
            pltpu.VMEM((2, n, BR), F32),
            pltpu.VMEM((2, n_ctx, BR), F32),
        ],
        compiler_params=_cparams(1),
        name="hgrn2",
    )(gain, *tables, slab, slab, g, slab_c, slab_c, g_c)


def _dft_tables(n):
    lo = 64
    hi = n // lo
    kk = np.arange(n, dtype=np.int64)
    ang_hi = 2.0 * np.pi * ((np.arange(hi, dtype=np.int64)[:, None] * lo * kk[None, :]) % n) / n
    ang_lo = 2.0 * np.pi * ((np.arange(lo, dtype=np.int64)[:, None] * kk[None, :]) % n) / n
    ch, sh = (jnp.asarray(f(ang_hi), F32)[:, None, :] for f in (np.cos, np.sin))
    cl, sl = (jnp.asarray(f(ang_lo), F32)[None, :, :] for f in (np.cos, np.sin))
    scale = 1.0 / math.sqrt(n)
    cos = ((ch * cl - sh * sl) * scale).reshape(n, n)
    sin = ((sh * cl + ch * sl) * scale).reshape(n, n)
    return jnp.concatenate([cos, -sin], axis=1).astype(BF16)


def _channel_dft_tables():
    j = np.arange(BR)
    same = (j[:, None] // FT_DG) == (j[None, :] // FT_DG)
    ang = 2.0 * np.pi * (((j[:, None] % FT_DG) * (j[None, :] % FT_DG)) % FT_DG) / FT_DG
    scale = 1.0 / math.sqrt(FT_DG)
    return (jnp.asarray(np.where(same, np.cos(ang), 0.0) * scale, BF16),
            jnp.asarray(np.where(same, np.sin(ang), 0.0) * scale, BF16))


def _fourier_kernel(cs_ref, bc_ref, bs_ref, u_ref, o_ref):
    u = u_ref[...]
    stacked = jnp.concatenate([_dot(u, bc_ref[...]).astype(BF16), _dot(u, bs_ref[...]).astype(BF16)], axis=0)
    o_ref[...] = _dot(cs_ref[...], stacked).astype(BF16)


def _fourier(cs, bc, bs, slab, *, n):
    batch = slab.shape[0] // n
    return pl.pallas_call(
        _fourier_kernel,
        grid=(batch,),
        in_specs=[
            _resident((n, 2 * n)),
            _resident((BR, BR)),
            _resident((BR, BR)),
            pl.BlockSpec((n, BR), lambda b: (b, COL_U)),
        ],
        out_specs=pl.BlockSpec((n, BR), lambda b: (b, 0)),
        out_shape=jax.ShapeDtypeStruct((batch * n, BR), BF16),
        compiler_params=_cparams(1),
        name="fourier",
    )(cs, bc, bs, slab)


def _merge_kernel(x_ref, mod_ref, ng_ref, ya_ref, yb_ref, yh_ref, yf_ref, sg_ref, wup_ref, wmg_ref, wout_ref, o_ref):
    x = x_ref[...]
    h = _modulated_norm(x, mod_ref, ng_ref).astype(BF16)
    acc = None
    for i, y_ref in enumerate((ya_ref, yb_ref, yh_ref, yf_ref)):
        y = (y_ref[...].astype(F32) * sg_ref[:, i * BR:(i + 1) * BR].astype(F32)).astype(BF16)
        term = jax.nn.sigmoid(_dot(h, wmg_ref[i])) * _dot(y, wup_ref[i])
        acc = term if acc is None else acc + term
    o_ref[...] = x + mod_ref[2:3, :] * _dot(acc.astype(BF16), wout_ref[...])


def _merge(x2d, mod_l, ng, ys, slab, wup, wmg, wout, *, tm, rows_per_mod, mod_row0):
    n = x2d.shape[0]

    def mod_map(t):
        return ((t * tm) // rows_per_mod + mod_row0, 0, 0)

    return pl.pallas_call(
        _merge_kernel,
        grid=(n // tm,),
        in_specs=[
            pl.BlockSpec((tm, D_MODEL), lambda t: (t, 0)),
            pl.BlockSpec((None, 3, D_MODEL), mod_map),
            _resident((1, D_MODEL)),
            *[pl.BlockSpec((tm, BR), lambda t: (t, 0)) for _ in range(4)],
            pl.BlockSpec((tm, 4 * BR), lambda t: (t, COL_SG)),
            _resident((4, BR, D_MODEL)),
            _resident((4, D_MODEL, D_MODEL)),
            _resident((D_MODEL, D_MODEL)),
        ],
        out_specs=pl.BlockSpec((tm, D_MODEL), lambda t: (t, 0)),
        out_shape=jax.ShapeDtypeStruct((n, D_MODEL), F32),
        compiler_params=_cparams(1),
        name="merge",
    )(x2d, mod_l, ng, *ys, slab, wup, wmg, wout)


def _rope_tables(n):
    t = jnp.arange(n)
    row = (t // GRID_W).astype(F32)
    col = (t % GRID_W).astype(F32)
    d_axis = DA_DH // 2
    inv = ROPE_THETA ** (-jnp.arange(0, d_axis, 2, dtype=F32) / d_axis)
    ang = jnp.concatenate([row[:, None] * inv, col[:, None] * inv], axis=-1)
    cos, sin = jnp.cos(ang), jnp.sin(ang)
    cos_h = jnp.concatenate([cos, cos], axis=-1)
    sin_h = jnp.concatenate([-sin, sin], axis=-1)
    return jnp.tile(cos_h, (1, 2 * DA_HEADS)), jnp.tile(sin_h, (1, 2 * DA_HEADS))


def kernel(x, c, ctx, c_ctx, norm_gain, w_mod, b_mod, w_in, da_qk_gain, da_lambda, da_subln_gain,
           na_qk_gain, na_rpb, hg_lb_logits, hg_norm_gain, w_up, w_merge, w_out):
    batch, n, _ = x.shape
    n_ctx = ctx.shape[1]
    assert n == 2048 and n % GRID_W == 0 and batch + 1 <= MOD_ROWS

    cc = jnp.concatenate([c, c_ctx[None, :], jnp.zeros((MOD_ROWS - batch - 1, D_MODEL), F32)], axis=0)
    mod = _modulation(cc, w_mod, b_mod).reshape(DEPTH, MOD_ROWS, 3, D_MODEL)

    p_lb = jax.nn.softmax(hg_lb_logits.astype(F32), axis=1)
    lower = jnp.cumsum(p_lb, axis=1) - p_lb[:, :1]
    lb_floored = jnp.maximum(lower, LB_FLOOR)
    one_minus_lb = 1.0 - lower
    lv = da_lambda.astype(F32)
    lam_all = jnp.exp(jnp.sum(lv[:, 0] * lv[:, 1], axis=-1)) - jnp.exp(jnp.sum(lv[:, 2] * lv[:, 3], axis=-1))

    cos, sin = _rope_tables(n)
    cs_lat, cs_ctx = _dft_tables(n), _dft_tables(n_ctx)
    bc, bs = _channel_dft_tables()
    bias_all = _na_bias_tables(na_rpb)
    hg_tables = _hgrn_tables()

    xl = x.reshape(batch * n, D_MODEL)
    xc = ctx.reshape(batch * n_ctx, D_MODEL)
    for l in range(DEPTH):
        need_ctx = l < DEPTH - 1
        lam_init = 0.8 - 0.6 * math.exp(-0.3 * l)
        lam = (lam_all[l] + lam_init).reshape(1)
        ng = norm_gain[l].reshape(1, D_MODEL)
        w = w_in[l].astype(BF16)
        gda = jnp.tile(da_qk_gain[l], (1, BR // DA_DH)) * jnp.array([[DA_DH ** -0.5 * LOG2E], [1.0]], F32)
        gna = jnp.tile(na_qk_gain[l], (1, BR // NA_DH)) * jnp.array([[NA_DH ** -0.5 * LOG2E], [1.0]], F32)
        lbp = jnp.stack([lb_floored[0, l], one_minus_lb[0, l], lb_floored[1, l], one_minus_lb[1, l]])
        subln = da_subln_gain[l].reshape(1, DA_DV)
        hgain = jnp.tile(hg_norm_gain[l].reshape(1, HG_D), (1, HG_HEADS))

        slab, g = _inproj(xl, mod[l], ng, w, gda, gna, lbp, cos, sin,
                          tm=512, rows_per_mod=n, mod_row0=0, rope=True)
        slab_c, g_c = _inproj(xc, mod[l], ng, w, gda, gna, lbp, cos[:n_ctx], sin[:n_ctx],
                              tm=n_ctx, rows_per_mod=batch * n_ctx, mod_row0=batch, rope=False)

        ya = _diff_attention(lam, subln, slab, [(slab, n), (slab_c, n_ctx)], nq=n, tq=n, lam_init=lam_init)
        yb = _neigh_attention(bias_all, l, slab, slab_c, n=n, n_ctx=n_ctx)
        yh, yh_c = _hgrn(hgain, hg_tables, slab, g, slab_c, g_c, n=n, n_ctx=n_ctx)
        yf = _fourier(cs_lat, bc, bs, slab, n=n)

        wup = w_up[l].astype(BF16)
        wmg = w_merge[l].astype(BF16)
        wout = w_out[l].astype(BF16)
        xl = _merge(xl, mod[l], ng, (ya, yb, yh, yf), slab, wup, wmg, wout,
                    tm=512, rows_per_mod=n, mod_row0=0)
        if need_ctx:
            ya_c = _diff_attention(lam, subln, slab_c, [(slab_c, n_ctx)], nq=n_ctx, tq=n_ctx, lam_init=lam_init)
            yb_c = _ctx_attention(slab_c, n_ctx=n_ctx)
            yf_c = _fourier(cs_ctx, bc, bs, slab_c, n=n_ctx)
            xc = _merge(xc, mod[l], ng, (ya_c, yb_c, yh_c, yf_c), slab_c, wup, wmg, wout,
                        tm=n_ctx, rows_per_mod=batch * n_ctx, mod_row0=batch)
    return xl.reshape(batch, n, D_MODEL)
```

```python
import functools
import math

import numpy as np
import jax
import jax.numpy as jnp
from jax import lax
from jax.experimental import pallas as pl
from jax.experimental.pallas import tpu as pltpu

D_MODEL = 1024
DEPTH = 4
GRID_W = 64
BR = 256
DA_HEADS, DA_DH, DA_DV = 4, 32, 64
NA_HEADS, NA_DH, NA_WIN_H, NA_WIN_W = 4, 64, 8, 16
HG_HEADS, HG_D, HG_CHUNK = 4, 64, 64
FT_DG = 64
IN_WIDTH = 15 * BR
LB_FLOOR = 1e-20
ROPE_THETA = 10000.0
EPS = 1e-6
NEG_INF = -1e30

COL_SG = 0
COL_QA, COL_KA, COL_QB, COL_KB = 4, 5, 6, 7
COL_VA, COL_VB = 4, 5
COL_HQ, COL_HI, COL_U = 12, 13, 14
SLAB_W = 15 * BR
LOG2E = 1.4426950408889634

DA_QUNIT = 512
NA_QROWS = 4
NA_KROWS = 12
MOD_ROWS = 24

VMEM_LIMIT_BYTES = 56 * 1024 * 1024

F32 = jnp.float32
BF16 = jnp.bfloat16


def _cparams(n_axes):
    return pltpu.CompilerParams(dimension_semantics=("arbitrary",) * n_axes, vmem_limit_bytes=VMEM_LIMIT_BYTES)


def _resident(shape):
    nd = len(shape)
    return pl.BlockSpec(shape, lambda *_: (0,) * nd, pipeline_mode=pl.Buffered(1))


def _dot(a, b):
    return jnp.dot(a, b, preferred_element_type=F32)


def _dot_nt(a, b):
    return lax.dot_general(a, b, (((1,), (1,)), ((), ())), preferred_element_type=F32)


def _dot_tn(a, b):
    return lax.dot_general(a, b, (((0,), (0,)), ((), ())), preferred_element_type=F32)


def _split3(v):
    hi = v.astype(BF16)
    r = v - hi.astype(F32)
    mid = r.astype(BF16)
    lo = (r - mid.astype(F32)).astype(BF16)
    return hi, mid, lo


def _group_mean(v, gsize, passes):
    w = v.shape[-1]
    r = lax.broadcasted_iota(jnp.int32, (w, w), 0) // gsize
    c = lax.broadcasted_iota(jnp.int32, (w, w), 1) // gsize
    ones = jnp.where(r == c, 1.0 / gsize, 0.0).astype(BF16)
    total, rest = None, v
    for _ in range(passes):
        piece = rest.astype(BF16)
        rest = rest - piece.astype(F32)
        part = _dot(piece, ones)
        total = part if total is None else total + part
    return total


def _group_rmsnorm(v, gain, gsize, passes=2):
    return v * lax.rsqrt(_group_mean(v * v, gsize, passes) + EPS) * gain


def _silu(v):
    return v * jax.nn.sigmoid(v)


def _mod_kernel(c_ref, w_ref, b_ref, o_ref):
    s = _silu(c_ref[...]).astype(BF16)
    o_ref[...] = _dot(s, w_ref[...].astype(BF16)) + b_ref[...]


def _modulation(cc, w_mod, b_mod):
    tn = D_MODEL
    return pl.pallas_call(
        _mod_kernel,
        grid=(DEPTH, 3 * D_MODEL // tn),
        in_specs=[
            pl.BlockSpec((MOD_ROWS, D_MODEL), lambda l, j: (0, 0)),
            pl.BlockSpec((None, D_MODEL, tn), lambda l, j: (l, 0, j)),
            pl.BlockSpec((None, 1, tn), lambda l, j: (l, 0, j)),
        ],
        out_specs=pl.BlockSpec((None, MOD_ROWS, tn), lambda l, j: (l, 0, j)),
        out_shape=jax.ShapeDtypeStruct((DEPTH, MOD_ROWS, 3 * D_MODEL), F32),
        compiler_params=_cparams(2),
        name="modulation",
    )(cc, w_mod, b_mod.reshape(DEPTH, 1, 3 * D_MODEL))


def _modulated_norm(x, mod_ref, ng_ref):
    ms = jnp.mean(x * x, axis=-1, keepdims=True)
    row_scale = ng_ref[...] * (1.0 + mod_ref[1:2, :])
    return x * lax.rsqrt(ms + EPS) * row_scale + mod_ref[0:1, :]


def _log_forget(f, lb_floored, one_minus_lb):
    return jnp.log(lb_floored + one_minus_lb * jax.nn.sigmoid(f))


def _rope(v, cos, sin_signed):
    lane = lax.broadcasted_iota(jnp.int32, v.shape, 1)
    first_half = (lane % DA_DH) < (DA_DH // 2)
    swapped = jnp.where(first_half, pltpu.roll(v, BR - DA_DH // 2, 1), pltpu.roll(v, DA_DH // 2, 1))
    return v * cos + swapped * sin_signed


def _inproj_kernel(x_ref, mod_ref, ng_ref, w_ref, gda_ref, gna_ref, lb_ref, cos_ref, sin_ref, slab_ref, g_ref, *, rope):
    h = _modulated_norm(x_ref[...], mod_ref, ng_ref).astype(BF16)

    def proj(col):
        return _dot(h, w_ref[:, col * BR:(col + 1) * BR])

    def put(col, v):
        slab_ref[:, col * BR:(col + 1) * BR] = v.astype(BF16)

    def put_values(col2, v):
        ones = jnp.ones((v.shape[0], 64), F32)
        pieces = []
        for hd in range(BR // 64):
            pieces += [v[:, hd * 64:(hd + 1) * 64], ones]
        slab_ref[:, col2 * 2 * BR:(col2 + 1) * 2 * BR] = jnp.concatenate(pieces, axis=-1).astype(BF16)

    g_ref[:, 0:BR] = _log_forget(proj(7), lb_ref[0:1, :], lb_ref[1:2, :])
    g_ref[:, BR:2 * BR] = _log_forget(proj(8), lb_ref[2:3, :], lb_ref[3:4, :])
    q = _group_rmsnorm(proj(0), gda_ref[0:1, :], DA_DH, passes=1)
    k = _group_rmsnorm(proj(1), gda_ref[1:2, :], DA_DH, passes=1)
    if rope:
        q = _rope(q, cos_ref[...], sin_ref[...])
        k = _rope(k, cos_ref[...], sin_ref[...])
    put(COL_QA, q)
    put(COL_KA, k)
    put(COL_QB, _group_rmsnorm(proj(3), gna_ref[0:1, :], NA_DH, passes=1))
    put(COL_KB, _group_rmsnorm(proj(4), gna_ref[1:2, :], NA_DH, passes=1))
    for i in range(4):
        put(COL_SG + i, _silu(proj(11 + i)))
    put(COL_HQ, _silu(proj(6)))
    put_values(COL_VA, proj(2))
    put_values(COL_VB, proj(5))
    put(COL_HI, proj(9))
    put(COL_U, proj(10))


def _inproj(x2d, mod_l, ng, w, gda, gna, lbp, cos, sin, *, tm, rows_per_mod, mod_row0, rope):
    n = x2d.shape[0]
    n_pos = cos.shape[0]
    tiles_per_seq = n_pos // tm

    def mod_map(t):
        return ((t * tm) // rows_per_mod + mod_row0, 0, 0)

    return pl.pallas_call(
        functools.partial(_inproj_kernel, rope=rope),
        grid=(n // tm,),
        in_specs=[
            pl.BlockSpec((tm, D_MODEL), lambda t: (t, 0)),
            pl.BlockSpec((None, 3, D_MODEL), mod_map),
            _resident((1, D_MODEL)),
            _resident((D_MODEL, IN_WIDTH)),
            _resident((2, BR)),
            _resident((2, BR)),
            _resident((4, BR)),
            pl.BlockSpec((tm, BR), lambda t: (t % tiles_per_seq, 0)),
            pl.BlockSpec((tm, BR), lambda t: (t % tiles_per_seq, 0)),
        ],
        out_specs=[
            pl.BlockSpec((tm, SLAB_W), lambda t: (t, 0)),
            pl.BlockSpec((tm, 2 * BR), lambda t: (t, 0)),
        ],
        out_shape=[
            jax.ShapeDtypeStruct((n, SLAB_W), BF16),
            jax.ShapeDtypeStruct((n, 2 * BR), F32),
        ],
        compiler_params=_cparams(1),
        name="inproj_rope" if rope else "inproj",
    )(x2d, mod_l, ng, w, gda, gna, lbp, cos, sin)


def _softmax_pv_scores(scores, values):
    m = scores[0].max(axis=-1, keepdims=True)
    for s in scores[1:]:
        m = jnp.maximum(m, s.max(axis=-1, keepdims=True))
    acc = None
    for s, v in zip(scores, values):
        p = _dot(jnp.exp2(s - m).astype(BF16), v)
        acc = p if acc is None else acc + p
    return (acc / pltpu.roll(acc, 64, 1))[:, :64]


def _softmax_pv(q, segments):
    return _softmax_pv_scores([_dot_nt(q, k) for k, _ in segments], [v for _, v in segments])


def _da_kernel(lam_ref, g_ref, q_ref, *refs, lam_init, n_seg):
    kv_refs, o_ref = refs[:2 * n_seg], refs[2 * n_seg]
    lam = lam_ref[0]
    for r0 in range(0, q_ref.shape[0], DA_QUNIT):
        rows = slice(r0, min(r0 + DA_QUNIT, q_ref.shape[0]))
        outs = []
        for h in range(DA_HEADS):
            vs = slice(h * 128, (h + 1) * 128)
            att = []
            for j in range(2):
                qs = slice((2 * h + j) * DA_DH, (2 * h + j + 1) * DA_DH)
                segs = [(kv_refs[2 * s][:, qs], kv_refs[2 * s + 1][:, vs]) for s in range(n_seg)]
                att.append(_softmax_pv(q_ref[rows, qs], segs))
            o = att[0] - lam * att[1]
            o = o * lax.rsqrt(jnp.mean(o * o, axis=-1, keepdims=True) + EPS) * g_ref[...] * (1.0 - lam_init)
            outs.append(o)
        o_ref[rows, :] = jnp.concatenate(outs, axis=-1).astype(BF16)


def _diff_attention(lam, subln, slab_q, key_slabs, *, nq, tq, lam_init):
    batch = slab_q.shape[0] // nq
    qb = nq // tq
    in_specs = [
        pl.BlockSpec(memory_space=pltpu.SMEM),
        _resident((1, DA_DV)),
        pl.BlockSpec((tq, BR), lambda b, i: (b * qb + i, COL_QA)),
    ]
    args = [lam, subln, slab_q]
    for slab, nk in key_slabs:
        in_specs += [pl.BlockSpec((nk, BR), lambda b, i: (b, COL_KA)),
                     pl.BlockSpec((nk, 2 * BR), lambda b, i: (b, COL_VA))]
        args += [slab, slab]
    return pl.pallas_call(
        functools.partial(_da_kernel, lam_init=lam_init, n_seg=len(key_slabs)),
        grid=(batch, qb),
        in_specs=in_specs,
        out_specs=pl.BlockSpec((tq, BR), lambda b, i: (b * qb + i, 0)),
        out_shape=jax.ShapeDtypeStruct((batch * nq, BR), BF16),
        compiler_params=_cparams(2),
        name="diff_attention",
    )(*args)


def _da_pipelined_kernel(lam_ref, g_ref, q_ref, kl_ref, kc_ref, vl_ref, vc_ref, o_ref, e_ref, *, lam_init):
    n_lat = kl_ref.shape[0]

    @pl.when(pl.program_id(0) == 0)
    def _():
        e_ref[...] = jnp.ones(e_ref.shape, BF16)

    lam = lam_ref[0]
    outs = []
    for h in range(DA_HEADS):
        vs = slice(h * 128, (h + 1) * 128)
        att = []
        for j in range(2):
            u = 2 * h + j
            qs = slice(u * DA_DH, (u + 1) * DA_DH)
            acc = _dot(e_ref[u, :, :n_lat], vl_ref[:, vs]) + _dot(e_ref[u, :, n_lat:], vc_ref[:, vs])
            att.append((acc / pltpu.roll(acc, 64, 1))[:, :64])
            q = q_ref[:, qs]
            s_l = _dot_nt(q, kl_ref[:, qs])
            s_c = _dot_nt(q, kc_ref[:, qs])
            m = jnp.maximum(s_l.max(axis=-1, keepdims=True), s_c.max(axis=-1, keepdims=True))
            e_ref[u, :, :n_lat] = jnp.exp2(s_l - m).astype(BF16)
            e_ref[u, :, n_lat:] = jnp.exp2(s_c - m).astype(BF16)
        o = att[0] - lam * att[1]
        o = o * lax.rsqrt(jnp.mean(o * o, axis=-1, keepdims=True) + EPS) * g_ref[...] * (1.0 - lam_init)
        outs.append(o)
    o_ref[...] = jnp.concatenate(outs, axis=-1).astype(BF16)


def _diff_attention_pipelined(lam, subln, slab, slab_c, *, n, n_ctx, tq, lam_init):
    qb = n // tq
    steps = slab.shape[0] // tq

    def cur(s):
        return jnp.minimum(s, steps - 1)

    def prev(s):
        return jnp.maximum(s - 1, 0)

    return pl.pallas_call(
        functools.partial(_da_pipelined_kernel, lam_init=lam_init),
        grid=(steps + 1,),
        in_specs=[
            pl.BlockSpec(memory_space=pltpu.SMEM),
            _resident((1, DA_DV)),
            pl.BlockSpec((tq, BR), lambda s: (cur(s), COL_QA)),
            pl.BlockSpec((n, BR), lambda s: (cur(s) // qb, COL_KA)),
            pl.BlockSpec((n_ctx, BR), lambda s: (cur(s) // qb, COL_KA)),
            pl.BlockSpec((n, 2 * BR), lambda s: (prev(s) // qb, COL_VA)),
            pl.BlockSpec((n_ctx, 2 * BR), lambda s: (prev(s) // qb, COL_VA)),
        ],
        out_specs=pl.BlockSpec((tq, BR), lambda s: (prev(s), 0)),
        out_shape=jax.ShapeDtypeStruct((slab.shape[0], BR), BF16),
        scratch_shapes=[pltpu.VMEM((2 * DA_HEADS, tq, n + n_ctx), BF16)],
        compiler_params=_cparams(1),
        name="diff_attention_pipelined",
    )(lam, subln, slab, slab, slab_c, slab, slab_c)


def _na_bias_tables(rpb):
    rows = 2048 // GRID_W
    n_dr, n_dc = 2 * NA_WIN_H - 1, 2 * NA_WIN_W - 1
    geoms = [(0, 0), (NA_QROWS, 0), (rows - NA_QROWS, rows - NA_KROWS)]
    qc = np.arange(GRID_W)[:, None]
    kc = np.arange(GRID_W)[None, :]
    c0 = np.clip(qc - NA_WIN_W // 2, 0, GRID_W - NA_WIN_W)
    col_ok = (kc >= c0) & (kc < c0 + NA_WIN_W)
    dc = np.clip(kc - qc, 1 - NA_WIN_W, NA_WIN_W - 1) + NA_WIN_W - 1
    col_onehot = (dc[None] == np.arange(n_dc)[:, None, None]).astype(np.float32)
    qr = np.arange(NA_QROWS)[:, None]
    kr = np.arange(NA_KROWS)[None, :]
    row_onehot = np.zeros((3, NA_QROWS, NA_KROWS, n_dr), np.float32)
    ok = np.zeros((3, NA_QROWS, GRID_W, NA_KROWS, GRID_W), bool)
    for gi, (q0, ws) in enumerate(geoms):
        r = q0 + qr
        r0 = np.clip(r - NA_WIN_H // 2, 0, rows - NA_WIN_H)
        kabs = ws + kr
        row_ok = (kabs >= r0) & (kabs < r0 + NA_WIN_H)
        dr = kabs - r + NA_WIN_H - 1
        row_onehot[gi] = (dr[..., None] == np.arange(n_dr)) & row_ok[..., None]
        ok[gi] = row_ok[:, None, :, None] & col_ok[None, :, None, :]
    by_col = jnp.einsum("lhrd,dqc->lhrqc", rpb.astype(F32), col_onehot, precision=lax.Precision.HIGHEST)
    bias = jnp.einsum("gakr,lhrqc->lghaqkc", row_onehot, by_col, precision=lax.Precision.HIGHEST)
    bias = jnp.where(ok[None, :, None], bias * LOG2E, NEG_INF)
    return bias.reshape(rpb.shape[0], 3, NA_HEADS, NA_QROWS * GRID_W, NA_KROWS * GRID_W)


def _na_kernel(bias_ref, q_ref, kl_ref, vl_ref, kc_ref, vc_ref, o_ref):
    tq = NA_QROWS * GRID_W
    nkw = NA_KROWS * GRID_W
    n_units = q_ref.shape[0] // tq
    last_ws = q_ref.shape[0] // GRID_W - NA_KROWS
    for i in range(n_units):
        ws = min(max(NA_QROWS * i - NA_WIN_H // 2, 0), last_ws) * GRID_W
        geom = 0 if i == 0 else (2 if i == n_units - 1 else 1)
        rows = slice(i * tq, (i + 1) * tq)
        win = slice(ws, ws + nkw)
        outs = []
        for h in range(NA_HEADS):
            hs = slice(h * NA_DH, (h + 1) * NA_DH)
            vs = slice(h * 128, (h + 1) * 128)
            q = q_ref[rows, hs]
            s_w = _dot_nt(q, kl_ref[win, hs]) + bias_ref[geom, h]
            s_c = _dot_nt(q, kc_ref[:, hs])
            outs.append(_softmax_pv_scores([s_w, s_c], [vl_ref[win, vs], vc_ref[:, vs]]))
        o_ref[rows, :] = jnp.concatenate(outs, axis=-1).astype(BF16)


def _neigh_attention(bias_all, layer, slab, slab_c, *, n, n_ctx):
    batch = slab.shape[0] // n
    return pl.pallas_call(
        _na_kernel,
        grid=(batch,),
        in_specs=[
            pl.BlockSpec((None,) + bias_all.shape[1:], lambda b: (layer, 0, 0, 0, 0), pipeline_mode=pl.Buffered(1)),
            pl.BlockSpec((n, BR), lambda b: (b, COL_QB)),
            pl.BlockSpec((n, BR), lambda b: (b, COL_KB)),
            pl.BlockSpec((n, 2 * BR), lambda b: (b, COL_VB)),
            pl.BlockSpec((n_ctx, BR), lambda b: (b, COL_KB)),
            pl.BlockSpec((n_ctx, 2 * BR), lambda b: (b, COL_VB)),
        ],
        out_specs=pl.BlockSpec((n, BR), lambda b: (b, 0)),
        out_shape=jax.ShapeDtypeStruct((batch * n, BR), BF16),
        compiler_params=_cparams(1),
        name="neigh_attention",
    )(bias_all, slab, slab, slab, slab_c, slab_c)


def _ctx_attn_kernel(q_ref, k_ref, v_ref, o_ref):
    outs = []
    for h in range(NA_HEADS):
        hs = slice(h * NA_DH, (h + 1) * NA_DH)
        outs.append(_softmax_pv(q_ref[:, hs], [(k_ref[:, hs], v_ref[:, h * 128:(h + 1) * 128])]))
    o_ref[...] = jnp.concatenate(outs, axis=-1).astype(BF16)


def _ctx_attention(slab_c, *, n_ctx):
    batch = slab_c.shape[0] // n_ctx
    return pl.pallas_call(
        _ctx_attn_kernel,
        grid=(batch,),
        in_specs=[
            pl.BlockSpec((n_ctx, BR), lambda b: (b, COL_QB)),
            pl.BlockSpec((n_ctx, BR), lambda b: (b, COL_KB)),
            pl.BlockSpec((n_ctx, 2 * BR), lambda b: (b, COL_VB)),
        ],
        out_specs=pl.BlockSpec((n_ctx, BR), lambda b: (b, 0)),
        out_shape=jax.ShapeDtypeStruct((batch * n_ctx, BR), BF16),
        compiler_params=_cparams(1),
        name="ctx_attention",
    )(slab_c, slab_c, slab_c)


HG_PAIR = 2 * HG_D
HG_LEVELS = (32, 16, 8, 4, 2, 1)
HG_UNROLL = 4


def _hgrn_tables():
    c = HG_CHUNK
    t = np.arange(c)[:, None]
    s = np.arange(HG_PAIR)[None, :] % c
    cum = np.stack([np.arange(c)[None, :] <= t, np.arange(c)[None, :] >= t]).astype(np.float32)
    side = np.zeros((2, len(HG_LEVELS), c, HG_PAIR), np.float32)
    out = np.zeros((2, len(HG_LEVELS) + 1, c, HG_PAIR), np.float32)
    for d in range(2):
        out[d, 0] = t == s
        for li, m in enumerate(HG_LEVELS):
            q_t = (t % (2 * m) >= m) if d == 0 else (t % (2 * m) < m)
            k_s = (s % (2 * m) < m) if d == 0 else (s % (2 * m) >= m)
            side[d, li] = np.broadcast_to(q_t, (c, HG_PAIR))
            out[d, li + 1] = (t // (2 * m) == s // (2 * m)) & q_t & k_s
    pos = np.broadcast_to(t % 4, (c, HG_PAIR))
    coef = np.zeros((2, 3, c, HG_PAIR), np.float32)
    coef[0, 0] = pos >= 2
    coef[1, 0] = pos < 2
    coef[:, 1] = pos == 0
    coef[:, 2] = pos == 3
    r = np.arange(HG_PAIR)
    bd = (r[:, None] // HG_D == r[None, :] // HG_D).astype(np.float32)
    return (jnp.asarray(cum, BF16), jnp.asarray(side), jnp.asarray(out), jnp.asarray(coef),
            jnp.asarray(bd, BF16), jnp.asarray(bd))


def _row_block_broadcast(b, size, offset):
    parts = [jnp.broadcast_to(b[s + offset:s + offset + 1, :], (size, b.shape[1])) for s in range(0, b.shape[0], size)]
    return parts[0] if len(parts) == 1 else jnp.concatenate(parts, axis=0)


def _hgrn_unit(q16, v16, g, d, cum_ref, side_ref, out_ref, coef_ref, bd16_ref, bd_ref):
    c = HG_CHUNK
    bd16 = bd16_ref[...]

    def block_diag(x16):
        return jnp.concatenate([x16, x16], axis=0) * bd16

    q = q16.astype(F32)
    g_hi = g.astype(BF16)
    g_lo = (g - g_hi.astype(F32)).astype(BF16)
    cum = cum_ref[d]
    b = _dot(cum, g_hi) + _dot(cum, g_lo)
    f = jnp.exp(g)
    k = 1.0 - f
    yield
    btot = b[0:1, :] if d == 1 else b[c - 1:c, :]

    def level_scores(li, x):
        x16 = x.astype(BF16)
        return out_ref[d, li + 1] * _dot_nt(x16, block_diag(x16))

    att = out_ref[d, 0] * _dot_nt(q16, block_diag(k.astype(BF16)))
    for li, m in enumerate(HG_LEVELS):
        is_q = side_ref[d, li] != 0.0
        if m >= 4:
            ref_b = _row_block_broadcast(b, 2 * m, m if d == 1 else m - 1)
            decay = jnp.exp(-jnp.abs(b - ref_b))
            x = jnp.where(is_q, q, k) * decay
        elif m == 2:
            e = coef_ref[d, 0] * g + coef_ref[d, 1] * pltpu.roll(g, c - 1, 0) + coef_ref[d, 2] * pltpu.roll(g, 1, 0)
            x = jnp.where(is_q, q, k) * jnp.exp(e)
        else:
            x = jnp.where(is_q, q * f, k)
        att = att + level_scores(li, x)
        yield

    o_intra = _dot(att.astype(BF16), block_diag(v16))
    q_in = (q * jnp.exp(b)).astype(BF16)
    yield
    upd = bd_ref[...] * _dot_tn(v16, (k * jnp.exp(btot - b)).astype(BF16))
    return o_intra, q_in, jnp.exp(btot), upd


def _hgrn_carry(st, o_intra, q_in, decay, upd):
    return o_intra + _dot_nt(q_in, st.astype(BF16)), st * decay + upd


def _interleave(units):
    results = [None] * len(units)
    live = list(range(len(units)))
    while live:
        for i in list(live):
            try:
                next(units[i])
            except StopIteration as done:
                results[i] = done.value
                live.remove(i)
    return results


def _hgrn_kernel(gain_ref, cum_ref, side_ref, out_ref, coef_ref, bd16_ref, bd_ref,
                 ql_ref, il_ref, gl_ref, qc_ref, ic_ref, gc_ref, yl_ref, yc_ref, st_ref, ol_ref, oc_ref):
    c = HG_CHUNK
    st_ref[...] = jnp.zeros(st_ref.shape, F32)

    def scan(q_ref, i_ref, g_ref, o_ref):
        n_chunks = q_ref.shape[0] // c
        pairs = [(d, p) for d in range(2) for p in range(BR // HG_PAIR)]

        def body(step, carry):
            where, units = [], []
            for u in range(HG_UNROLL):
                for d, p in pairs:
                    idx = step * HG_UNROLL + u
                    rows = pl.ds(pl.multiple_of((idx if d == 0 else n_chunks - 1 - idx) * c, c), c)
                    lanes = slice(p * HG_PAIR, (p + 1) * HG_PAIR)
                    g_lanes = slice(d * BR + p * HG_PAIR, d * BR + (p + 1) * HG_PAIR)
                    where.append((d, rows, lanes))
                    units.append(_hgrn_unit(q_ref[rows, lanes], i_ref[rows, lanes], g_ref[rows, g_lanes],
                                            d, cum_ref, side_ref, out_ref, coef_ref, bd16_ref, bd_ref))
            parts = _interleave(units)
            states = [st_ref[d, p] for d, p in pairs]
            for j, ((d, rows, lanes), part) in enumerate(zip(where, parts)):
                o, states[j % len(pairs)] = _hgrn_carry(states[j % len(pairs)], *part)
                o_ref[d, rows, lanes] = o
            for (d, p), st in zip(pairs, states):
                st_ref[d, p] = st
            return carry

        lax.fori_loop(0, n_chunks // HG_UNROLL, body, 0)

    scan(qc_ref, ic_ref, gc_ref, oc_ref)
    scan(ql_ref, il_ref, gl_ref, ol_ref)
    yl_ref[...] = _group_rmsnorm(ol_ref[0] + ol_ref[1], gain_ref[...], HG_D).astype(BF16)
    yc_ref[...] = _group_rmsnorm(oc_ref[0] + oc_ref[1], gain_ref[...], HG_D).astype(BF16)


def _hgrn(gain, tables, slab, g, slab_c, g_c, *, n, n_ctx):
    batch = slab.shape[0] // n
    return pl.pallas_call(
        _hgrn_kernel,
        grid=(batch,),
        in_specs=[
            _resident((1, BR)),
            *[_resident(t.shape) for t in tables],
            pl.BlockSpec((n, BR), lambda b: (b, COL_HQ)),
            pl.BlockSpec((n, BR), lambda b: (b, COL_HI)),
            pl.BlockSpec((n, 2 * BR), lambda b: (b, 0)),
            pl.BlockSpec((n_ctx, BR), lambda b: (b, COL_HQ)),
            pl.BlockSpec((n_ctx, BR), lambda b: (b, COL_HI)),
            pl.BlockSpec((n_ctx, 2 * BR), lambda b: (b, 0)),
        ],
        out_specs=[
            pl.BlockSpec((n, BR), lambda b: (b, 0)),
            pl.BlockSpec((n_ctx, BR), lambda b: (b, 0)),
        ],
        out_shape=[
            jax.ShapeDtypeStruct((batch * n, BR), BF16),
            jax.ShapeDtypeStruct((batch * n_ctx, BR), BF16),
        ],
        scratch_shapes=[
            pltpu.VMEM((2, BR // HG_PAIR, HG_PAIR, HG_PAIR), F32),
            pltpu.VMEM((2, n, BR), F32),
            pltpu.VMEM((2, n_ctx, BR), F32),
        ],
        compiler_params=_cparams(1),
        name="hgrn2",
    )(gain, *tables, slab, slab, g, slab_c, slab_c, g_c)


def _dft_tables(n):
    lo = 64
    hi = n // lo
    kk = np.arange(n, dtype=np.int64)
    ang_hi = 2.0 * np.pi * ((np.arange(hi, dtype=np.int64)[:, None] * lo * kk[None, :]) % n) / n
    ang_lo = 2.0 * np.pi * ((np.arange(lo, dtype=np.int64)[:, None] * kk[None, :]) % n) / n
    ch, sh = (jnp.asarray(f(ang_hi), F32)[:, None, :] for f in (np.cos, np.sin))
    cl, sl = (jnp.asarray(f(ang_lo), F32)[None, :, :] for f in (np.cos, np.sin))
    scale = 1.0 / math.sqrt(n)
    cos = ((ch * cl - sh * sl) * scale).reshape(n, n)
    sin = ((sh * cl + ch * sl) * scale).reshape(n, n)
    return jnp.concatenate([cos, -sin], axis=1).astype(BF16)


def _channel_dft_tables():
    j = np.arange(BR)
    same = (j[:, None] // FT_DG) == (j[None, :] // FT_DG)
    ang = 2.0 * np.pi * (((j[:, None] % FT_DG) * (j[None, :] % FT_DG)) % FT_DG) / FT_DG
    scale = 1.0 / math.sqrt(FT_DG)
    return (jnp.asarray(np.where(same, np.cos(ang), 0.0) * scale, BF16),
            jnp.asarray(np.where(same, np.sin(ang), 0.0) * scale, BF16))


def _fourier_kernel(cs_ref, bc_ref, bs_ref, u_ref, o_ref):
    u = u_ref[...]
    stacked = jnp.concatenate([_dot(u, bc_ref[...]).astype(BF16), _dot(u, bs_ref[...]).astype(BF16)], axis=0)
    o_ref[...] = _dot(cs_ref[...], stacked).astype(BF16)


def _fourier(cs, bc, bs, slab, *, n):
    batch = slab.shape[0] // n
    return pl.pallas_call(
        _fourier_kernel,
        grid=(batch,),
        in_specs=[
            _resident((n, 2 * n)),
            _resident((BR, BR)),
            _resident((BR, BR)),
            pl.BlockSpec((n, BR), lambda b: (b, COL_U)),
        ],
        out_specs=pl.BlockSpec((n, BR), lambda b: (b, 0)),
        out_shape=jax.ShapeDtypeStruct((batch * n, BR), BF16),
        compiler_params=_cparams(1),
        name="fourier",
    )(cs, bc, bs, slab)


def _merge_kernel(x_ref, mod_ref, ng_ref, ya_ref, yb_ref, yh_ref, yf_ref, sg_ref, wup_ref, wmg_ref, wout_ref, o_ref):
    x = x_ref[...]
    h = _modulated_norm(x, mod_ref, ng_ref).astype(BF16)
    acc = None
    for i, y_ref in enumerate((ya_ref, yb_ref, yh_ref, yf_ref)):
        y = (y_ref[...].astype(F32) * sg_ref[:, i * BR:(i + 1) * BR].astype(F32)).astype(BF16)
        term = jax.nn.sigmoid(_dot(h, wmg_ref[i])) * _dot(y, wup_ref[i])
        acc = term if acc is None else acc + term
    o_ref[...] = x + mod_ref[2:3, :] * _dot(acc.astype(BF16), wout_ref[...])


def _merge(x2d, mod_l, ng, ys, slab, wup, wmg, wout, *, tm, rows_per_mod, mod_row0):
    n = x2d.shape[0]

    def mod_map(t):
        return ((t * tm) // rows_per_mod + mod_row0, 0, 0)

    return pl.pallas_call(
        _merge_kernel,
        grid=(n // tm,),
        in_specs=[
            pl.BlockSpec((tm, D_MODEL), lambda t: (t, 0)),
            pl.BlockSpec((None, 3, D_MODEL), mod_map),
            _resident((1, D_MODEL)),
            *[pl.BlockSpec((tm, BR), lambda t: (t, 0)) for _ in range(4)],
            pl.BlockSpec((tm, 4 * BR), lambda t: (t, COL_SG)),
            _resident((4, BR, D_MODEL)),
            _resident((4, D_MODEL, D_MODEL)),
            _resident((D_MODEL, D_MODEL)),
        ],
        out_specs=pl.BlockSpec((tm, D_MODEL), lambda t: (t, 0)),
        out_shape=jax.ShapeDtypeStruct((n, D_MODEL), F32),
        compiler_params=_cparams(1),
        name="merge",
    )(x2d, mod_l, ng, *ys, slab, wup, wmg, wout)


def _rope_tables(n):
    t = jnp.arange(n)
    row = (t // GRID_W).astype(F32)
    col = (t % GRID_W).astype(F32)
    d_axis = DA_DH // 2
    inv = ROPE_THETA ** (-jnp.arange(0, d_axis, 2, dtype=F32) / d_axis)
    ang = jnp.concatenate([row[:, None] * inv, col[:, None] * inv], axis=-1)
    cos, sin = jnp.cos(ang), jnp.sin(ang)
    cos_h = jnp.concatenate([cos, cos], axis=-1)
    sin_h = jnp.concatenate([-sin, sin], axis=-1)
    return jnp.tile(cos_h, (1, 2 * DA_HEADS)), jnp.tile(sin_h, (1, 2 * DA_HEADS))


def kernel(x, c, ctx, c_ctx, norm_gain, w_mod, b_mod, w_in, da_qk_gain, da_lambda, da_subln_gain,
           na_qk_gain, na_rpb, hg_lb_logits, hg_norm_gain, w_up, w_merge, w_out):
    batch, n, _ = x.shape
    n_ctx = ctx.shape[1]
    assert n == 2048 and n % GRID_W == 0 and batch + 1 <= MOD_ROWS

    cc = jnp.concatenate([c, c_ctx[None, :], jnp.zeros((MOD_ROWS - batch - 1, D_MODEL), F32)], axis=0)
    mod = _modulation(cc, w_mod, b_mod).reshape(DEPTH, MOD_ROWS, 3, D_MODEL)

    p_lb = jax.nn.softmax(hg_lb_logits.astype(F32), axis=1)
    lower = jnp.cumsum(p_lb, axis=1) - p_lb[:, :1]
    lb_floored = jnp.maximum(lower, LB_FLOOR)
    one_minus_lb = 1.0 - lower
    lv = da_lambda.astype(F32)
    lam_all = jnp.exp(jnp.sum(lv[:, 0] * lv[:, 1], axis=-1)) - jnp.exp(jnp.sum(lv[:, 2] * lv[:, 3], axis=-1))

    cos, sin = _rope_tables(n)
    cs_lat, cs_ctx = _dft_tables(n), _dft_tables(n_ctx)
    bc, bs = _channel_dft_tables()
    bias_all = _na_bias_tables(na_rpb)
    hg_tables = _hgrn_tables()

    xl = x.reshape(batch * n, D_MODEL)
    xc = ctx.reshape(batch * n_ctx, D_MODEL)
    for l in range(DEPTH):
        need_ctx = l < DEPTH - 1
        lam_init = 0.8 - 0.6 * math.exp(-0.3 * l)
        lam = (lam_all[l] + lam_init).reshape(1)
        ng = norm_gain[l].reshape(1, D_MODEL)
        w = w_in[l].astype(BF16)
        gda = jnp.tile(da_qk_gain[l], (1, BR // DA_DH)) * jnp.array([[DA_DH ** -0.5 * LOG2E], [1.0]], F32)
        gna = jnp.tile(na_qk_gain[l], (1, BR // NA_DH)) * jnp.array([[NA_DH ** -0.5 * LOG2E], [1.0]], F32)
        lbp = jnp.stack([lb_floored[0, l], one_minus_lb[0, l], lb_floored[1, l], one_minus_lb[1, l]])
        subln = da_subln_gain[l].reshape(1, DA_DV)
        hgain = jnp.tile(hg_norm_gain[l].reshape(1, HG_D), (1, HG_HEADS))

        slab, g = _inproj(xl, mod[l], ng, w, gda, gna, lbp, cos, sin,
                          tm=512, rows_per_mod=n, mod_row0=0, rope=True)
        slab_c, g_c = _inproj(xc, mod[l], ng, w, gda, gna, lbp, cos[:n_ctx], sin[:n_ctx],
                              tm=n_ctx, rows_per_mod=batch * n_ctx, mod_row0=batch, rope=False)

        ya = _diff_attention_pipelined(lam, subln, slab, slab_c, n=n, n_ctx=n_ctx, tq=DA_QUNIT, lam_init=lam_init)
        yb = _neigh_attention(bias_all, l, slab, slab_c, n=n, n_ctx=n_ctx)
        yh, yh_c = _hgrn(hgain, hg_tables, slab, g, slab_c, g_c, n=n, n_ctx=n_ctx)
        yf = _fourier(cs_lat, bc, bs, slab, n=n)

        wup = w_up[l].astype(BF16)
        wmg = w_merge[l].astype(BF16)
        wout = w_out[l].astype(BF16)
        xl = _merge(xl, mod[l], ng, (ya, yb, yh, yf), slab, wup, wmg, wout,
                    tm=512, rows_per_mod=n, mod_row0=0)
        if need_ctx:
            ya_c = _diff_attention(lam, subln, slab_c, [(slab_c, n_ctx)], nq=n_ctx, tq=n_ctx, lam_init=lam_init)
            yb_c = _ctx_attention(slab_c, n_ctx=n_ctx)
            yf_c = _fourier(cs_ctx, bc, bs, slab_c, n=n_ctx)
            xc = _merge(xc, mod[l], ng, (ya_c, yb_c, yh_c, yf_c), slab_c, wup, wmg, wout,
                        tm=n_ctx, rows_per_mod=batch * n_ctx, mod_row0=batch)
    return xl.reshape(batch, n, D_MODEL)
```

```python
import functools
import math

import numpy as np
import jax
import jax.numpy as jnp
from jax import lax
from jax.experimental import pallas as pl
from jax.experimental.pallas import tpu as pltpu

D_MODEL = 1024
DEPTH = 4
GRID_W = 64
BR = 256
DA_HEADS, DA_DH, DA_DV = 4, 32, 64
NA_HEADS, NA_DH, NA_WIN_H, NA_WIN_W = 4, 64, 8, 16
HG_HEADS, HG_D, HG_CHUNK = 4, 64, 64
FT_DG = 64
IN_WIDTH = 15 * BR
LB_FLOOR = 1e-20
ROPE_THETA = 10000.0
EPS = 1e-6
NEG_INF = -1e30

COL_SG = 0
COL_QA, COL_KA, COL_QB, COL_KB = 4, 5, 6, 7
COL_VA, COL_VB = 4, 5
COL_HQ, COL_HI, COL_U = 12, 13, 14
SLAB_W = 15 * BR
LOG2E = 1.4426950408889634

DA_QUNIT = 512
NA_QROWS = 4
NA_KROWS = 12
MOD_ROWS = 24

VMEM_LIMIT_BYTES = 56 * 1024 * 1024

F32 = jnp.float32
BF16 = jnp.bfloat16


def _cparams(n_axes):
    return pltpu.CompilerParams(dimension_semantics=("arbitrary",) * n_axes, vmem_limit_bytes=VMEM_LIMIT_BYTES)


def _resident(shape):
    nd = len(shape)
    return pl.BlockSpec(shape, lambda *_: (0,) * nd, pipeline_mode=pl.Buffered(1))


def _dot(a, b):
    return jnp.dot(a, b, preferred_element_type=F32)


def _dot_nt(a, b):
    return lax.dot_general(a, b, (((1,), (1,)), ((), ())), preferred_element_type=F32)


def _dot_tn(a, b):
    return lax.dot_general(a, b, (((0,), (0,)), ((), ())), preferred_element_type=F32)


def _split3(v):
    hi = v.astype(BF16)
    r = v - hi.astype(F32)
    mid = r.astype(BF16)
    lo = (r - mid.astype(F32)).astype(BF16)
    return hi, mid, lo


def _group_mean(v, gsize, passes):
    w = v.shape[-1]
    r = lax.broadcasted_iota(jnp.int32, (w, w), 0) // gsize
    c = lax.broadcasted_iota(jnp.int32, (w, w), 1) // gsize
    ones = jnp.where(r == c, 1.0 / gsize, 0.0).astype(BF16)
    total, rest = None, v
    for _ in range(passes):
        piece = rest.astype(BF16)
        rest = rest - piece.astype(F32)
        part = _dot(piece, ones)
        total = part if total is None else total + part
    return total


def _group_rmsnorm(v, gain, gsize, passes=2):
    return v * lax.rsqrt(_group_mean(v * v, gsize, passes) + EPS) * gain


def _silu(v):
    return v * jax.nn.sigmoid(v)


def _mod_kernel(c_ref, w_ref, b_ref, o_ref):
    s = _silu(c_ref[...]).astype(BF16)
    o_ref[...] = _dot(s, w_ref[...].astype(BF16)) + b_ref[...]


def _modulation(cc, w_mod, b_mod):
    tn = D_MODEL
    return pl.pallas_call(
        _mod_kernel,
        grid=(DEPTH, 3 * D_MODEL // tn),
        in_specs=[
            pl.BlockSpec((MOD_ROWS, D_MODEL), lambda l, j: (0, 0)),
            pl.BlockSpec((None, D_MODEL, tn), lambda l, j: (l, 0, j)),
            pl.BlockSpec((None, 1, tn), lambda l, j: (l, 0, j)),
        ],
        out_specs=pl.BlockSpec((None, MOD_ROWS, tn), lambda l, j: (l, 0, j)),
        out_shape=jax.ShapeDtypeStruct((DEPTH, MOD_ROWS, 3 * D_MODEL), F32),
        compiler_params=_cparams(2),
        name="modulation",
    )(cc, w_mod, b_mod.reshape(DEPTH, 1, 3 * D_MODEL))


def _modulated_norm(x, mod_ref, ng_ref):
    ms = jnp.mean(x * x, axis=-1, keepdims=True)
    row_scale = ng_ref[...] * (1.0 + mod_ref[1:2, :])
    return x * lax.rsqrt(ms + EPS) * row_scale + mod_ref[0:1, :]


def _log_forget(f, lb_floored, one_minus_lb):
    return jnp.log(lb_floored + one_minus_lb * jax.nn.sigmoid(f))


def _rope(v, cos, sin_signed):
    lane = lax.broadcasted_iota(jnp.int32, v.shape, 1)
    first_half = (lane % DA_DH) < (DA_DH // 2)
    swapped = jnp.where(first_half, pltpu.roll(v, BR - DA_DH // 2, 1), pltpu.roll(v, DA_DH // 2, 1))
    return v * cos + swapped * sin_signed


def _inproj_kernel(x_ref, mod_ref, ng_ref, w_ref, gda_ref, gna_ref, lb_ref, cos_ref, sin_ref, slab_ref, g_ref, *, rope):
    h = _modulated_norm(x_ref[...], mod_ref, ng_ref).astype(BF16)

    def proj(col):
        return _dot(h, w_ref[:, col * BR:(col + 1) * BR])

    def put(col, v):
        slab_ref[:, col * BR:(col + 1) * BR] = v.astype(BF16)

    def put_values(col2, v):
        ones = jnp.ones((v.shape[0], 64), F32)
        pieces = []
        for hd in range(BR // 64):
            pieces += [v[:, hd * 64:(hd + 1) * 64], ones]
        slab_ref[:, col2 * 2 * BR:(col2 + 1) * 2 * BR] = jnp.concatenate(pieces, axis=-1).astype(BF16)

    raw_qa, raw_ka, raw_qb, raw_kb = proj(0), proj(1), proj(3), proj(4)
    g_ref[:, 0:BR] = _log_forget(proj(7), lb_ref[0:1, :], lb_ref[1:2, :])
    g_ref[:, BR:2 * BR] = _log_forget(proj(8), lb_ref[2:3, :], lb_ref[3:4, :])
    q = _group_rmsnorm(raw_qa, gda_ref[0:1, :], DA_DH, passes=1)
    k = _group_rmsnorm(raw_ka, gda_ref[1:2, :], DA_DH, passes=1)
    if rope:
        q = _rope(q, cos_ref[...], sin_ref[...])
        k = _rope(k, cos_ref[...], sin_ref[...])
    put(COL_QA, q)
    put(COL_KA, k)
    put(COL_QB, _group_rmsnorm(raw_qb, gna_ref[0:1, :], NA_DH, passes=1))
    put(COL_KB, _group_rmsnorm(raw_kb, gna_ref[1:2, :], NA_DH, passes=1))
    for i in range(4):
        put(COL_SG + i, _silu(proj(11 + i)))
    put(COL_HQ, _silu(proj(6)))
    put_values(COL_VA, proj(2))
    put_values(COL_VB, proj(5))
    put(COL_HI, proj(9))
    put(COL_U, proj(10))


def _inproj(x2d, mod_l, ng, w, gda, gna, lbp, cos, sin, *, tm, rows_per_mod, mod_row0, rope):
    n = x2d.shape[0]
    n_pos = cos.shape[0]
    tiles_per_seq = n_pos // tm

    def mod_map(t):
        return ((t * tm) // rows_per_mod + mod_row0, 0, 0)

    return pl.pallas_call(
        functools.partial(_inproj_kernel, rope=rope),
        grid=(n // tm,),
        in_specs=[
            pl.BlockSpec((tm, D_MODEL), lambda t: (t, 0)),
            pl.BlockSpec((None, 3, D_MODEL), mod_map),
            _resident((1, D_MODEL)),
            _resident((D_MODEL, IN_WIDTH)),
            _resident((2, BR)),
            _resident((2, BR)),
            _resident((4, BR)),
            pl.BlockSpec((tm, BR), lambda t: (t % tiles_per_seq, 0)),
            pl.BlockSpec((tm, BR), lambda t: (t % tiles_per_seq, 0)),
        ],
        out_specs=[
            pl.BlockSpec((tm, SLAB_W), lambda t: (t, 0)),
            pl.BlockSpec((tm, 2 * BR), lambda t: (t, 0)),
        ],
        out_shape=[
            jax.ShapeDtypeStruct((n, SLAB_W), BF16),
            jax.ShapeDtypeStruct((n, 2 * BR), F32),
        ],
        compiler_params=_cparams(1),
        name="inproj_rope" if rope else "inproj",
    )(x2d, mod_l, ng, w, gda, gna, lbp, cos, sin)


def _softmax_pv_scores(scores, values):
    m = scores[0].max(axis=-1, keepdims=True)
    for s in scores[1:]:
        m = jnp.maximum(m, s.max(axis=-1, keepdims=True))
    acc = None
    for s, v in zip(scores, values):
        p = _dot(jnp.exp2(s - m).astype(BF16), v)
        acc = p if acc is None else acc + p
    return (acc / pltpu.roll(acc, 64, 1))[:, :64]


def _softmax_pv(q, segments):
    return _softmax_pv_scores([_dot_nt(q, k) for k, _ in segments], [v for _, v in segments])


def _da_kernel(lam_ref, g_ref, q_ref, *refs, lam_init, n_seg):
    kv_refs, o_ref = refs[:2 * n_seg], refs[2 * n_seg]
    lam = lam_ref[0]
    for r0 in range(0, q_ref.shape[0], DA_QUNIT):
        rows = slice(r0, min(r0 + DA_QUNIT, q_ref.shape[0]))
        outs = []
        for h in range(DA_HEADS):
            vs = slice(h * 128, (h + 1) * 128)
            att = []
            for j in range(2):
                qs = slice((2 * h + j) * DA_DH, (2 * h + j + 1) * DA_DH)
                segs = [(kv_refs[2 * s][:, qs], kv_refs[2 * s + 1][:, vs]) for s in range(n_seg)]
                att.append(_softmax_pv(q_ref[rows, qs], segs))
            o = att[0] - lam * att[1]
            o = o * lax.rsqrt(jnp.mean(o * o, axis=-1, keepdims=True) + EPS) * g_ref[...] * (1.0 - lam_init)
            outs.append(o)
        o_ref[rows, :] = jnp.concatenate(outs, axis=-1).astype(BF16)


def _diff_attention(lam, subln, slab_q, key_slabs, *, nq, tq, lam_init):
    batch = slab_q.shape[0] // nq
    qb = nq // tq
    in_specs = [
        pl.BlockSpec(memory_space=pltpu.SMEM),
        _resident((1, DA_DV)),
        pl.BlockSpec((tq, BR), lambda b, i: (b * qb + i, COL_QA)),
    ]
    args = [lam, subln, slab_q]
    for slab, nk in key_slabs:
        in_specs += [pl.BlockSpec((nk, BR), lambda b, i: (b, COL_KA)),
                     pl.BlockSpec((nk, 2 * BR), lambda b, i: (b, COL_VA))]
        args += [slab, slab]
    return pl.pallas_call(
        functools.partial(_da_kernel, lam_init=lam_init, n_seg=len(key_slabs)),
        grid=(batch, qb),
        in_specs=in_specs,
        out_specs=pl.BlockSpec((tq, BR), lambda b, i: (b * qb + i, 0)),
        out_shape=jax.ShapeDtypeStruct((batch * nq, BR), BF16),
        compiler_params=_cparams(2),
        name="diff_attention",
    )(*args)


def _da_pipelined_kernel(lam_ref, g_ref, q_ref, kl_ref, kc_ref, vl_ref, vc_ref, o_ref, e_ref, *, lam_init):
    n_lat = kl_ref.shape[0]

    @pl.when(pl.program_id(0) == 0)
    def _():
        e_ref[...] = jnp.ones(e_ref.shape, BF16)

    lam = lam_ref[0]
    outs = []
    for h in range(DA_HEADS):
        vs = slice(h * 128, (h + 1) * 128)
        att = []
        for j in range(2):
            u = 2 * h + j
            qs = slice(u * DA_DH, (u + 1) * DA_DH)
            acc = _dot(e_ref[u, :, :n_lat], vl_ref[:, vs]) + _dot(e_ref[u, :, n_lat:], vc_ref[:, vs])
            att.append((acc / pltpu.roll(acc, 64, 1))[:, :64])
            q = q_ref[:, qs]
            s_l = _dot_nt(q, kl_ref[:, qs])
            s_c = _dot_nt(q, kc_ref[:, qs])
            m = jnp.maximum(s_l.max(axis=-1, keepdims=True), s_c.max(axis=-1, keepdims=True))
            e_ref[u, :, :n_lat] = jnp.exp2(s_l - m).astype(BF16)
            e_ref[u, :, n_lat:] = jnp.exp2(s_c - m).astype(BF16)
        o = att[0] - lam * att[1]
        o = o * lax.rsqrt(jnp.mean(o * o, axis=-1, keepdims=True) + EPS) * g_ref[...] * (1.0 - lam_init)
        outs.append(o)
    o_ref[...] = jnp.concatenate(outs, axis=-1).astype(BF16)


def _diff_attention_pipelined(lam, subln, slab, slab_c, *, n, n_ctx, tq, lam_init):
    qb = n // tq
    steps = slab.shape[0] // tq

    def cur(s):
        return jnp.minimum(s, steps - 1)

    def prev(s):
        return jnp.maximum(s - 1, 0)

    return pl.pallas_call(
        functools.partial(_da_pipelined_kernel, lam_init=lam_init),
        grid=(steps + 1,),
        in_specs=[
            pl.BlockSpec(memory_space=pltpu.SMEM),
            _resident((1, DA_DV)),
            pl.BlockSpec((tq, BR), lambda s: (cur(s), COL_QA)),
            pl.BlockSpec((n, BR), lambda s: (cur(s) // qb, COL_KA)),
            pl.BlockSpec((n_ctx, BR), lambda s: (cur(s) // qb, COL_KA)),
            pl.BlockSpec((n, 2 * BR), lambda s: (prev(s) // qb, COL_VA)),
            pl.BlockSpec((n_ctx, 2 * BR), lambda s: (prev(s) // qb, COL_VA)),
        ],
        out_specs=pl.BlockSpec((tq, BR), lambda s: (prev(s), 0)),
        out_shape=jax.ShapeDtypeStruct((slab.shape[0], BR), BF16),
        scratch_shapes=[pltpu.VMEM((2 * DA_HEADS, tq, n + n_ctx), BF16)],
        compiler_params=_cparams(1),
        name="diff_attention_pipelined",
    )(lam, subln, slab, slab, slab_c, slab, slab_c)


def _na_bias_tables(rpb):
    rows = 2048 // GRID_W
    n_dc = 2 * NA_WIN_W - 1
    geoms = [(0, 0), (NA_QROWS, 0), (rows - NA_QROWS, rows - NA_KROWS)]
    qc = np.arange(GRID_W)[:, None]
    kc = np.arange(GRID_W)[None, :]
    c0 = np.clip(qc - NA_WIN_W // 2, 0, GRID_W - NA_WIN_W)
    col_ok = (kc >= c0) & (kc < c0 + NA_WIN_W)
    dc = np.clip(kc - qc, 1 - NA_WIN_W, NA_WIN_W - 1) + NA_WIN_W - 1
    col_onehot = (dc[None] == np.arange(n_dc)[:, None, None]).astype(np.float32)
    by_col = jnp.einsum("lhrd,dqc->lhrqc", rpb.astype(F32), col_onehot, precision=lax.Precision.HIGHEST)
    by_col = jnp.where(col_ok, by_col * LOG2E, NEG_INF)
    masked = jnp.full(by_col.shape[:2] + (GRID_W, GRID_W), NEG_INF, F32)
    tables = []
    for q0, ws in geoms:
        per_qrow = []
        for a in range(NA_QROWS):
            r = q0 + a
            r0 = min(max(r - NA_WIN_H // 2, 0), rows - NA_WIN_H)
            tiles = [by_col[:, :, ws + j - r + NA_WIN_H - 1] if r0 <= ws + j < r0 + NA_WIN_H else masked
                     for j in range(NA_KROWS)]
            per_qrow.append(jnp.concatenate(tiles, axis=-1))
        tables.append(jnp.concatenate(per_qrow, axis=2))
    return jnp.stack(tables, axis=1)


def _na_kernel(bias_ref, q_ref, kl_ref, vl_ref, kc_ref, vc_ref, o_ref):
    tq = NA_QROWS * GRID_W
    nkw = NA_KROWS * GRID_W
    n_units = q_ref.shape[0] // tq
    last_ws = q_ref.shape[0] // GRID_W - NA_KROWS
    def window(i):
        ws = min(max(NA_QROWS * i - NA_WIN_H // 2, 0), last_ws) * GRID_W
        return slice(ws, ws + nkw)

    def scores(i, h):
        hs = slice(h * NA_DH, (h + 1) * NA_DH)
        geom = 0 if i == 0 else (2 if i == n_units - 1 else 1)
        q = q_ref[i * tq:(i + 1) * tq, hs]
        return [_dot_nt(q, kl_ref[window(i), hs]) + bias_ref[geom, h], _dot_nt(q, kc_ref[:, hs])]

    units = [(i, h) for i in range(n_units) for h in range(NA_HEADS)]
    ready = scores(*units[0])
    outs = []
    for u, (i, h) in enumerate(units):
        upcoming = scores(*units[u + 1]) if u + 1 < len(units) else None
        vs = slice(h * 128, (h + 1) * 128)
        outs.append(_softmax_pv_scores(ready, [vl_ref[window(i), vs], vc_ref[:, vs]]))
        ready = upcoming
        if h == NA_HEADS - 1:
            o_ref[i * tq:(i + 1) * tq, :] = jnp.concatenate(outs, axis=-1).astype(BF16)
            outs = []


def _neigh_attention(bias_all, layer, slab, slab_c, *, n, n_ctx):
    batch = slab.shape[0] // n
    return pl.pallas_call(
        _na_kernel,
        grid=(batch,),
        in_specs=[
            pl.BlockSpec((None,) + bias_all.shape[1:], lambda b: (layer, 0, 0, 0, 0), pipeline_mode=pl.Buffered(1)),
            pl.BlockSpec((n, BR), lambda b: (b, COL_QB)),
            pl.BlockSpec((n, BR), lambda b: (b, COL_KB)),
            pl.BlockSpec((n, 2 * BR), lambda b: (b, COL_VB)),
            pl.BlockSpec((n_ctx, BR), lambda b: (b, COL_KB)),
            pl.BlockSpec((n_ctx, 2 * BR), lambda b: (b, COL_VB)),
        ],
        out_specs=pl.BlockSpec((n, BR), lambda b: (b, 0)),
        out_shape=jax.ShapeDtypeStruct((batch * n, BR), BF16),
        compiler_params=_cparams(1),
        name="neigh_attention",
    )(bias_all, slab, slab, slab, slab_c, slab_c)


def _ctx_attn_kernel(q_ref, k_ref, v_ref, o_ref):
    outs = []
    for h in range(NA_HEADS):
        hs = slice(h * NA_DH, (h + 1) * NA_DH)
        outs.append(_softmax_pv(q_ref[:, hs], [(k_ref[:, hs], v_ref[:, h * 128:(h + 1) * 128])]))
    o_ref[...] = jnp.concatenate(outs, axis=-1).astype(BF16)


def _ctx_attention(slab_c, *, n_ctx):
    batch = slab_c.shape[0] // n_ctx
    return pl.pallas_call(
        _ctx_attn_kernel,
        grid=(batch,),
        in_specs=[
            pl.BlockSpec((n_ctx, BR), lambda b: (b, COL_QB)),
            pl.BlockSpec((n_ctx, BR), lambda b: (b, COL_KB)),
            pl.BlockSpec((n_ctx, 2 * BR), lambda b: (b, COL_VB)),
        ],
        out_specs=pl.BlockSpec((n_ctx, BR), lambda b: (b, 0)),
        out_shape=jax.ShapeDtypeStruct((batch * n_ctx, BR), BF16),
        compiler_params=_cparams(1),
        name="ctx_attention",
    )(slab_c, slab_c, slab_c)


HG_PAIR = 2 * HG_D
HG_LEVELS = (32, 16, 8, 4, 2, 1)
HG_UNROLL = 4


def _hgrn_tables():
    c = HG_CHUNK
    t = np.arange(c)[:, None]
    s = np.arange(HG_PAIR)[None, :] % c
    cum = np.stack([np.arange(c)[None, :] <= t, np.arange(c)[None, :] >= t]).astype(np.float32)
    side = np.zeros((2, len(HG_LEVELS), c, HG_PAIR), np.float32)
    out = np.zeros((2, len(HG_LEVELS) + 1, c, HG_PAIR), np.float32)
    for d in range(2):
        out[d, 0] = t == s
        for li, m in enumerate(HG_LEVELS):
            q_t = (t % (2 * m) >= m) if d == 0 else (t % (2 * m) < m)
            k_s = (s % (2 * m) < m) if d == 0 else (s % (2 * m) >= m)
            side[d, li] = np.broadcast_to(q_t, (c, HG_PAIR))
            out[d, li + 1] = (t // (2 * m) == s // (2 * m)) & q_t & k_s
    pos = np.broadcast_to(t % 4, (c, HG_PAIR))
    coef = np.zeros((2, 3, c, HG_PAIR), np.float32)
    coef[0, 0] = pos >= 2
    coef[1, 0] = pos < 2
    coef[:, 1] = pos == 0
    coef[:, 2] = pos == 3
    r = np.arange(HG_PAIR)
    bd = (r[:, None] // HG_D == r[None, :] // HG_D).astype(np.float32)
    return (jnp.asarray(cum, BF16), jnp.asarray(side), jnp.asarray(out), jnp.asarray(coef),
            jnp.asarray(bd, BF16), jnp.asarray(bd))


def _row_block_broadcast(b, size, offset):
    parts = [jnp.broadcast_to(b[s + offset:s + offset + 1, :], (size, b.shape[1])) for s in range(0, b.shape[0], size)]
    return parts[0] if len(parts) == 1 else jnp.concatenate(parts, axis=0)


def _hgrn_unit(q16, v16, g, d, cum_ref, side_ref, out_ref, coef_ref, bd16_ref, bd_ref):
    c = HG_CHUNK
    bd16 = bd16_ref[...]

    def block_diag(x16):
        return jnp.concatenate([x16, x16], axis=0) * bd16

    q = q16.astype(F32)
    g_hi = g.astype(BF16)
    g_lo = (g - g_hi.astype(F32)).astype(BF16)
    cum = cum_ref[d]
    b = _dot(cum, g_hi) + _dot(cum, g_lo)
    f = jnp.exp(g)
    k = 1.0 - f
    yield
    btot = b[0:1, :] if d == 1 else b[c - 1:c, :]

    def level_scores(li, x):
        x16 = x.astype(BF16)
        return out_ref[d, li + 1] * _dot_nt(x16, block_diag(x16))

    att = out_ref[d, 0] * _dot_nt(q16, block_diag(k.astype(BF16)))
    for li, m in enumerate(HG_LEVELS):
        is_q = side_ref[d, li] != 0.0
        if m >= 4:
            ref_b = _row_block_broadcast(b, 2 * m, m if d == 1 else m - 1)
            decay = jnp.exp(-jnp.abs(b - ref_b))
            x = jnp.where(is_q, q, k) * decay
        elif m == 2:
            e = coef_ref[d, 0] * g + coef_ref[d, 1] * pltpu.roll(g, c - 1, 0) + coef_ref[d, 2] * pltpu.roll(g, 1, 0)
            x = jnp.where(is_q, q, k) * jnp.exp(e)
        else:
            x = jnp.where(is_q, q * f, k)
        att = att + level_scores(li, x)
        yield

    o_intra = _dot(att.astype(BF16), block_diag(v16))
    q_in = (q * jnp.exp(b)).astype(BF16)
    yield
    upd = bd_ref[...] * _dot_tn(v16, (k * jnp.exp(btot - b)).astype(BF16))
    return o_intra, q_in, jnp.exp(btot), upd


def _hgrn_carry(st, o_intra, q_in, decay, upd):
    return o_intra + _dot_nt(q_in, st.astype(BF16)), st * decay + upd


def _interleave(units):
    results = [None] * len(units)
    live = list(range(len(units)))
    while live:
        for i in list(live):
            try:
                next(units[i])
            except StopIteration as done:
                results[i] = done.value
                live.remove(i)
    return results


def _hgrn_kernel(gain_ref, cum_ref, side_ref, out_ref, coef_ref, bd16_ref, bd_ref,
                 ql_ref, il_ref, gl_ref, qc_ref, ic_ref, gc_ref, yl_ref, yc_ref, st_ref, ol_ref, oc_ref):
    c = HG_CHUNK
    st_ref[...] = jnp.zeros(st_ref.shape, F32)

    def scan(q_ref, i_ref, g_ref, o_ref):
        n_chunks = q_ref.shape[0] // c
        pairs = [(d, p) for d in range(2) for p in range(BR // HG_PAIR)]

        def body(step, carry):
            where, units = [], []
            for u in range(HG_UNROLL):
                for d, p in pairs:
                    idx = step * HG_UNROLL + u
                    rows = pl.ds(pl.multiple_of((idx if d == 0 else n_chunks - 1 - idx) * c, c), c)
                    lanes = slice(p * HG_PAIR, (p + 1) * HG_PAIR)
                    g_lanes = slice(d * BR + p * HG_PAIR, d * BR + (p + 1) * HG_PAIR)
                    where.append((d, rows, lanes))
                    units.append(_hgrn_unit(q_ref[rows, lanes], i_ref[rows, lanes], g_ref[rows, g_lanes],
                                            d, cum_ref, side_ref, out_ref, coef_ref, bd16_ref, bd_ref))
            parts = _interleave(units)
            states = [st_ref[d, p] for d, p in pairs]
            for j, ((d, rows, lanes), part) in enumerate(zip(where, parts)):
                o, states[j % len(pairs)] = _hgrn_carry(states[j % len(pairs)], *part)
                o_ref[d, rows, lanes] = o
            for (d, p), st in zip(pairs, states):
                st_ref[d, p] = st
            return carry

        lax.fori_loop(0, n_chunks // HG_UNROLL, body, 0)

    scan(qc_ref, ic_ref, gc_ref, oc_ref)
    scan(ql_ref, il_ref, gl_ref, ol_ref)
    yl_ref[...] = _group_rmsnorm(ol_ref[0] + ol_ref[1], gain_ref[...], HG_D).astype(BF16)
    yc_ref[...] = _group_rmsnorm(oc_ref[0] + oc_ref[1], gain_ref[...], HG_D).astype(BF16)


def _hgrn(gain, tables, slab, g, slab_c, g_c, *, n, n_ctx):
    batch = slab.shape[0] // n
    return pl.pallas_call(
        _hgrn_kernel,
        grid=(batch,),
        in_specs=[
            _resident((1, BR)),
            *[_resident(t.shape) for t in tables],
            pl.BlockSpec((n, BR), lambda b: (b, COL_HQ)),
            pl.BlockSpec((n, BR), lambda b: (b, COL_HI)),
            pl.BlockSpec((n, 2 * BR), lambda b: (b, 0)),
            pl.BlockSpec((n_ctx, BR), lambda b: (b, COL_HQ)),
            pl.BlockSpec((n_ctx, BR), lambda b: (b, COL_HI)),
            pl.BlockSpec((n_ctx, 2 * BR), lambda b: (b, 0)),
        ],
        out_specs=[
            pl.BlockSpec((n, BR), lambda b: (b, 0)),
            pl.BlockSpec((n_ctx, BR), lambda b: (b, 0)),
        ],
        out_shape=[
            jax.ShapeDtypeStruct((batch * n, BR), BF16),
            jax.ShapeDtypeStruct((batch * n_ctx, BR), BF16),
        ],
        scratch_shapes=[
            pltpu.VMEM((2, BR // HG_PAIR, HG_PAIR, HG_PAIR), F32),
            pltpu.VMEM((2, n, BR), F32),
            pltpu.VMEM((2, n_ctx, BR), F32),
        ],
        compiler_params=_cparams(1),
        name="hgrn2",
    )(gain, *tables, slab, slab, g, slab_c, slab_c, g_c)


def _dft_tables(n):
    lo = 64
    hi = n // lo
    kk = np.arange(n, dtype=np.int64)
    ang_hi = 2.0 * np.pi * ((np.arange(hi, dtype=np.int64)[:, None] * lo * kk[None, :]) % n) / n
    ang_lo = 2.0 * np.pi * ((np.arange(lo, dtype=np.int64)[:, None] * kk[None, :]) % n) / n
    ch, sh = (jnp.asarray(f(ang_hi), F32)[:, None, :] for f in (np.cos, np.sin))
    cl, sl = (jnp.asarray(f(ang_lo), F32)[None, :, :] for f in (np.cos, np.sin))
    scale = 1.0 / math.sqrt(n)
    cos = ((ch * cl - sh * sl) * scale).reshape(n, n)
    sin = ((sh * cl + ch * sl) * scale).reshape(n, n)
    return jnp.concatenate([cos, -sin], axis=1).astype(BF16)


def _channel_dft_tables():
    j = np.arange(BR)
    same = (j[:, None] // FT_DG) == (j[None, :] // FT_DG)
    ang = 2.0 * np.pi * (((j[:, None] % FT_DG) * (j[None, :] % FT_DG)) % FT_DG) / FT_DG
    scale = 1.0 / math.sqrt(FT_DG)
    return (jnp.asarray(np.where(same, np.cos(ang), 0.0) * scale, BF16),
            jnp.asarray(np.where(same, np.sin(ang), 0.0) * scale, BF16))


def _fourier_kernel(cs_ref, bc_ref, bs_ref, u_ref, o_ref):
    u = u_ref[...]
    stacked = jnp.concatenate([_dot(u, bc_ref[...]).astype(BF16), _dot(u, bs_ref[...]).astype(BF16)], axis=0)
    o_ref[...] = _dot(cs_ref[...], stacked).astype(BF16)


def _fourier(cs, bc, bs, slab, *, n):
    batch = slab.shape[0] // n
    return pl.pallas_call(
        _fourier_kernel,
        grid=(batch,),
        in_specs=[
            _resident((n, 2 * n)),
            _resident((BR, BR)),
            _resident((BR, BR)),
            pl.BlockSpec((n, BR), lambda b: (b, COL_U)),
        ],
        out_specs=pl.BlockSpec((n, BR), lambda b: (b, 0)),
        out_shape=jax.ShapeDtypeStruct((batch * n, BR), BF16),
        compiler_params=_cparams(1),
        name="fourier",
    )(cs, bc, bs, slab)


def _merge_kernel(x_ref, mod_ref, ng_ref, ya_ref, yb_ref, yh_ref, yf_ref, sg_ref, wup_ref, wmg_ref, wout_ref, o_ref):
    x = x_ref[...]
    h = _modulated_norm(x, mod_ref, ng_ref).astype(BF16)
    acc = None
    for i, y_ref in enumerate((ya_ref, yb_ref, yh_ref, yf_ref)):
        y = (y_ref[...].astype(F32) * sg_ref[:, i * BR:(i + 1) * BR].astype(F32)).astype(BF16)
        term = jax.nn.sigmoid(_dot(h, wmg_ref[i])) * _dot(y, wup_ref[i])
        acc = term if acc is None else acc + term
    o_ref[...] = x + mod_ref[2:3, :] * _dot(acc.astype(BF16), wout_ref[...])


def _merge(x2d, mod_l, ng, ys, slab, wup, wmg, wout, *, tm, rows_per_mod, mod_row0):
    n = x2d.shape[0]

    def mod_map(t):
        return ((t * tm) // rows_per_mod + mod_row0, 0, 0)

    return pl.pallas_call(
        _merge_kernel,
        grid=(n // tm,),
        in_specs=[
            pl.BlockSpec((tm, D_MODEL), lambda t: (t, 0)),
            pl.BlockSpec((None, 3, D_MODEL), mod_map),
            _resident((1, D_MODEL)),
            *[pl.BlockSpec((tm, BR), lambda t: (t, 0)) for _ in range(4)],
            pl.BlockSpec((tm, 4 * BR), lambda t: (t, COL_SG)),
            _resident((4, BR, D_MODEL)),
            _resident((4, D_MODEL, D_MODEL)),
            _resident((D_MODEL, D_MODEL)),
        ],
        out_specs=pl.BlockSpec((tm, D_MODEL), lambda t: (t, 0)),
        out_shape=jax.ShapeDtypeStruct((n, D_MODEL), F32),
        compiler_params=_cparams(1),
        name="merge",
    )(x2d, mod_l, ng, *ys, slab, wup, wmg, wout)


def _rope_tables(n):
    t = jnp.arange(n)
    row = (t // GRID_W).astype(F32)
    col = (t % GRID_W).astype(F32)
    d_axis = DA_DH // 2
    inv = ROPE_THETA ** (-jnp.arange(0, d_axis, 2, dtype=F32) / d_axis)
    ang = jnp.concatenate([row[:, None] * inv, col[:, None] * inv], axis=-1)
    cos, sin = jnp.cos(ang), jnp.sin(ang)
    cos_h = jnp.concatenate([cos, cos], axis=-1)
    sin_h = jnp.concatenate([-sin, sin], axis=-1)
    return jnp.tile(cos_h, (1, 2 * DA_HEADS)), jnp.tile(sin_h, (1, 2 * DA_HEADS))


def kernel(x, c, ctx, c_ctx, norm_gain, w_mod, b_mod, w_in, da_qk_gain, da_lambda, da_subln_gain,
           na_qk_gain, na_rpb, hg_lb_logits, hg_norm_gain, w_up, w_merge, w_out):
    batch, n, _ = x.shape
    n_ctx = ctx.shape[1]
    assert n == 2048 and n % GRID_W == 0 and batch + 1 <= MOD_ROWS

    cc = jnp.concatenate([c, c_ctx[None, :], jnp.zeros((MOD_ROWS - batch - 1, D_MODEL), F32)], axis=0)
    mod = _modulation(cc, w_mod, b_mod).reshape(DEPTH, MOD_ROWS, 3, D_MODEL)

    p_lb = jax.nn.softmax(hg_lb_logits.astype(F32), axis=1)
    lower = jnp.cumsum(p_lb, axis=1) - p_lb[:, :1]
    lb_floored = jnp.maximum(lower, LB_FLOOR)
    one_minus_lb = 1.0 - lower
    lv = da_lambda.astype(F32)
    lam_all = jnp.exp(jnp.sum(lv[:, 0] * lv[:, 1], axis=-1)) - jnp.exp(jnp.sum(lv[:, 2] * lv[:, 3], axis=-1))

    cos, sin = _rope_tables(n)
    cs_lat, cs_ctx = _dft_tables(n), _dft_tables(n_ctx)
    bc, bs = _channel_dft_tables()
    bias_all = _na_bias_tables(na_rpb)
    hg_tables = _hgrn_tables()

    xl = x.reshape(batch * n, D_MODEL)
    xc = ctx.reshape(batch * n_ctx, D_MODEL)
    for l in range(DEPTH):
        need_ctx = l < DEPTH - 1
        lam_init = 0.8 - 0.6 * math.exp(-0.3 * l)
        lam = (lam_all[l] + lam_init).reshape(1)
        ng = norm_gain[l].reshape(1, D_MODEL)
        w = w_in[l].astype(BF16)
        gda = jnp.tile(da_qk_gain[l], (1, BR // DA_DH)) * jnp.array([[DA_DH ** -0.5 * LOG2E], [1.0]], F32)
        gna = jnp.tile(na_qk_gain[l], (1, BR // NA_DH)) * jnp.array([[NA_DH ** -0.5 * LOG2E], [1.0]], F32)
        lbp = jnp.stack([lb_floored[0, l], one_minus_lb[0, l], lb_floored[1, l], one_minus_lb[1, l]])
        subln = da_subln_gain[l].reshape(1, DA_DV)
        hgain = jnp.tile(hg_norm_gain[l].reshape(1, HG_D), (1, HG_HEADS))

        slab, g = _inproj(xl, mod[l], ng, w, gda, gna, lbp, cos, sin,
                          tm=512, rows_per_mod=n, mod_row0=0, rope=True)
        slab_c, g_c = _inproj(xc, mod[l], ng, w, gda, gna, lbp, cos[:n_ctx], sin[:n_ctx],
                              tm=n_ctx, rows_per_mod=batch * n_ctx, mod_row0=batch, rope=False)

        ya = _diff_attention_pipelined(lam, subln, slab, slab_c, n=n, n_ctx=n_ctx, tq=DA_QUNIT, lam_init=lam_init)
        yb = _neigh_attention(bias_all, l, slab, slab_c, n=n, n_ctx=n_ctx)
        yh, yh_c = _hgrn(hgain, hg_tables, slab, g, slab_c, g_c, n=n, n_ctx=n_ctx)
        yf = _fourier(cs_lat, bc, bs, slab, n=n)

        wup = w_up[l].astype(BF16)
        wmg = w_merge[l].astype(BF16)
        wout = w_out[l].astype(BF16)
        xl = _merge(xl, mod[l], ng, (ya, yb, yh, yf), slab, wup, wmg, wout,
                    tm=512, rows_per_mod=n, mod_row0=0)
        if need_ctx:
            ya_c = _diff_attention(lam, subln, slab_c, [(slab_c, n_ctx)], nq=n_ctx, tq=n_ctx, lam_init=lam_init)
            yb_c = _ctx_attention(slab_c, n_ctx=n_ctx)
            yf_c = _fourier(cs_ctx, bc, bs, slab_c, n=n_ctx)
            xc = _merge(xc, mod[l], ng, (ya_c, yb_c, yh_c, yf_c), slab_c, wup, wmg, wout,
                        tm=n_ctx, rows_per_mod=batch * n_ctx, mod_row0=batch)
    return xl.reshape(batch, n, D_MODEL)
```

```python
import functools
import math

import numpy as np
import jax
import jax.numpy as jnp
from jax import lax
from jax.experimental import pallas as pl
from jax.experimental.pallas import tpu as pltpu

D_MODEL = 1024
DEPTH = 4
GRID_W = 64
BR = 256
DA_HEADS, DA_DH, DA_DV = 4, 32, 64
NA_HEADS, NA_DH, NA_WIN_H, NA_WIN_W = 4, 64, 8, 16
HG_HEADS, HG_D, HG_CHUNK = 4, 64, 64
FT_DG = 64
IN_WIDTH = 15 * BR
LB_FLOOR = 1e-20
ROPE_THETA = 10000.0
EPS = 1e-6
NEG_INF = -1e30

COL_SG = 0
COL_QA, COL_KA, COL_QB, COL_KB = 4, 5, 6, 7
COL_VA, COL_VB = 4, 5
COL_HQ, COL_HI, COL_U = 12, 13, 14
SLAB_W = 15 * BR
LOG2E = 1.4426950408889634

DA_QUNIT = 512
NA_QROWS = 4
NA_KROWS = 12
MOD_ROWS = 24

VMEM_LIMIT_BYTES = 56 * 1024 * 1024

F32 = jnp.float32
BF16 = jnp.bfloat16


def _cparams(n_axes):
    return pltpu.CompilerParams(dimension_semantics=("arbitrary",) * n_axes, vmem_limit_bytes=VMEM_LIMIT_BYTES)


def _resident(shape):
    nd = len(shape)
    return pl.BlockSpec(shape, lambda *_: (0,) * nd, pipeline_mode=pl.Buffered(1))


def _dot(a, b):
    return jnp.dot(a, b, preferred_element_type=F32)


def _dot_nt(a, b):
    return lax.dot_general(a, b, (((1,), (1,)), ((), ())), preferred_element_type=F32)


def _dot_tn(a, b):
    return lax.dot_general(a, b, (((0,), (0,)), ((), ())), preferred_element_type=F32)


def _split3(v):
    hi = v.astype(BF16)
    r = v - hi.astype(F32)
    mid = r.astype(BF16)
    lo = (r - mid.astype(F32)).astype(BF16)
    return hi, mid, lo


def _group_mean(v, gsize, passes):
    w = v.shape[-1]
    r = lax.broadcasted_iota(jnp.int32, (w, w), 0) // gsize
    c = lax.broadcasted_iota(jnp.int32, (w, w), 1) // gsize
    ones = jnp.where(r == c, 1.0 / gsize, 0.0).astype(BF16)
    total, rest = None, v
    for _ in range(passes):
        piece = rest.astype(BF16)
        rest = rest - piece.astype(F32)
        part = _dot(piece, ones)
        total = part if total is None else total + part
    return total


def _group_rmsnorm(v, gain, gsize, passes=2):
    return v * lax.rsqrt(_group_mean(v * v, gsize, passes) + EPS) * gain


def _silu(v):
    return v * jax.nn.sigmoid(v)


def _mod_kernel(c_ref, w_ref, b_ref, o_ref):
    s = _silu(c_ref[...]).astype(BF16)
    o_ref[...] = _dot(s, w_ref[...].astype(BF16)) + b_ref[...]


def _modulation(cc, w_mod, b_mod):
    tn = D_MODEL
    return pl.pallas_call(
        _mod_kernel,
        grid=(DEPTH, 3 * D_MODEL // tn),
        in_specs=[
            pl.BlockSpec((MOD_ROWS, D_MODEL), lambda l, j: (0, 0)),
            pl.BlockSpec((None, D_MODEL, tn), lambda l, j: (l, 0, j)),
            pl.BlockSpec((None, 1, tn), lambda l, j: (l, 0, j)),
        ],
        out_specs=pl.BlockSpec((None, MOD_ROWS, tn), lambda l, j: (l, 0, j)),
        out_shape=jax.ShapeDtypeStruct((DEPTH, MOD_ROWS, 3 * D_MODEL), F32),
        compiler_params=_cparams(2),
        name="modulation",
    )(cc, w_mod, b_mod.reshape(DEPTH, 1, 3 * D_MODEL))


def _modulated_norm(x, mod_ref, ng_ref):
    ms = jnp.mean(x * x, axis=-1, keepdims=True)
    row_scale = ng_ref[...] * (1.0 + mod_ref[1:2, :])
    return x * lax.rsqrt(ms + EPS) * row_scale + mod_ref[0:1, :]


def _log_forget(f, lb_floored, one_minus_lb):
    return jnp.log(lb_floored + one_minus_lb * jax.nn.sigmoid(f))


def _rope(v, cos, sin_signed):
    lane = lax.broadcasted_iota(jnp.int32, v.shape, 1)
    first_half = (lane % DA_DH) < (DA_DH // 2)
    swapped = jnp.where(first_half, pltpu.roll(v, BR - DA_DH // 2, 1), pltpu.roll(v, DA_DH // 2, 1))
    return v * cos + swapped * sin_signed


def _inproj_kernel(x_ref, mod_ref, ng_ref, w_ref, gda_ref, gna_ref, lb_ref, cos_ref, sin_ref, slab_ref, g_ref, *, rope):
    h = _modulated_norm(x_ref[...], mod_ref, ng_ref).astype(BF16)

    def proj(col):
        return _dot(h, w_ref[:, col * BR:(col + 1) * BR])

    def put(col, v):
        slab_ref[:, col * BR:(col + 1) * BR] = v.astype(BF16)

    def put_values(col2, v):
        ones = jnp.ones((v.shape[0], 64), F32)
        pieces = []
        for hd in range(BR // 64):
            pieces += [v[:, hd * 64:(hd + 1) * 64], ones]
        slab_ref[:, col2 * 2 * BR:(col2 + 1) * 2 * BR] = jnp.concatenate(pieces, axis=-1).astype(BF16)

    raw_qa, raw_ka, raw_qb, raw_kb = proj(0), proj(1), proj(3), proj(4)
    g_ref[:, 0:BR] = _log_forget(proj(7), lb_ref[0:1, :], lb_ref[1:2, :])
    g_ref[:, BR:2 * BR] = _log_forget(proj(8), lb_ref[2:3, :], lb_ref[3:4, :])
    q = _group_rmsnorm(raw_qa, gda_ref[0:1, :], DA_DH, passes=1)
    k = _group_rmsnorm(raw_ka, gda_ref[1:2, :], DA_DH, passes=1)
    if rope:
        q = _rope(q, cos_ref[...], sin_ref[...])
        k = _rope(k, cos_ref[...], sin_ref[...])
    put(COL_QA, q)
    put(COL_KA, k)
    put(COL_QB, _group_rmsnorm(raw_qb, gna_ref[0:1, :], NA_DH, passes=1))
    put(COL_KB, _group_rmsnorm(raw_kb, gna_ref[1:2, :], NA_DH, passes=1))
    for i in range(4):
        put(COL_SG + i, _silu(proj(11 + i)))
    put(COL_HQ, _silu(proj(6)))
    put_values(COL_VA, proj(2))
    put_values(COL_VB, proj(5))
    put(COL_HI, proj(9))
    put(COL_U, proj(10))


def _inproj(x2d, mod_l, ng, w, gda, gna, lbp, cos, sin, *, tm, rows_per_mod, mod_row0, rope):
    n = x2d.shape[0]
    n_pos = cos.shape[0]
    tiles_per_seq = n_pos // tm

    def mod_map(t):
        return ((t * tm) // rows_per_mod + mod_row0, 0, 0)

    return pl.pallas_call(
        functools.partial(_inproj_kernel, rope=rope),
        grid=(n // tm,),
        in_specs=[
            pl.BlockSpec((tm, D_MODEL), lambda t: (t, 0)),
            pl.BlockSpec((None, 3, D_MODEL), mod_map),
            _resident((1, D_MODEL)),
            _resident((D_MODEL, IN_WIDTH)),
            _resident((2, BR)),
            _resident((2, BR)),
            _resident((4, BR)),
            pl.BlockSpec((tm, BR), lambda t: (t % tiles_per_seq, 0)),
            pl.BlockSpec((tm, BR), lambda t: (t % tiles_per_seq, 0)),
        ],
        out_specs=[
            pl.BlockSpec((tm, SLAB_W), lambda t: (t, 0)),
            pl.BlockSpec((tm, 2 * BR), lambda t: (t, 0)),
        ],
        out_shape=[
            jax.ShapeDtypeStruct((n, SLAB_W), BF16),
            jax.ShapeDtypeStruct((n, 2 * BR), F32),
        ],
        compiler_params=_cparams(1),
        name="inproj_rope" if rope else "inproj",
    )(x2d, mod_l, ng, w, gda, gna, lbp, cos, sin)


def _softmax_pv_scores(scores, values):
    m = scores[0].max(axis=-1, keepdims=True)
    for s in scores[1:]:
        m = jnp.maximum(m, s.max(axis=-1, keepdims=True))
    acc = None
    for s, v in zip(scores, values):
        p = _dot(jnp.exp2(s - m).astype(BF16), v)
        acc = p if acc is None else acc + p
    return (acc / pltpu.roll(acc, 64, 1))[:, :64]


def _softmax_pv(q, segments):
    return _softmax_pv_scores([_dot_nt(q, k) for k, _ in segments], [v for _, v in segments])


def _da_kernel(lam_ref, g_ref, q_ref, *refs, lam_init, n_seg):
    kv_refs, o_ref = refs[:2 * n_seg], refs[2 * n_seg]
    lam = lam_ref[0]
    for r0 in range(0, q_ref.shape[0], DA_QUNIT):
        rows = slice(r0, min(r0 + DA_QUNIT, q_ref.shape[0]))
        outs = []
        for h in range(DA_HEADS):
            vs = slice(h * 128, (h + 1) * 128)
            att = []
            for j in range(2):
                qs = slice((2 * h + j) * DA_DH, (2 * h + j + 1) * DA_DH)
                segs = [(kv_refs[2 * s][:, qs], kv_refs[2 * s + 1][:, vs]) for s in range(n_seg)]
                att.append(_softmax_pv(q_ref[rows, qs], segs))
            o = att[0] - lam * att[1]
            o = o * lax.rsqrt(jnp.mean(o * o, axis=-1, keepdims=True) + EPS) * g_ref[...] * (1.0 - lam_init)
            outs.append(o)
        o_ref[rows, :] = jnp.concatenate(outs, axis=-1).astype(BF16)


def _diff_attention(lam, subln, slab_q, key_slabs, *, nq, tq, lam_init):
    batch = slab_q.shape[0] // nq
    qb = nq // tq
    in_specs = [
        pl.BlockSpec(memory_space=pltpu.SMEM),
        _resident((1, DA_DV)),
        pl.BlockSpec((tq, BR), lambda b, i: (b * qb + i, COL_QA)),
    ]
    args = [lam, subln, slab_q]
    for slab, nk in key_slabs:
        in_specs += [pl.BlockSpec((nk, BR), lambda b, i: (b, COL_KA)),
                     pl.BlockSpec((nk, 2 * BR), lambda b, i: (b, COL_VA))]
        args += [slab, slab]
    return pl.pallas_call(
        functools.partial(_da_kernel, lam_init=lam_init, n_seg=len(key_slabs)),
        grid=(batch, qb),
        in_specs=in_specs,
        out_specs=pl.BlockSpec((tq, BR), lambda b, i: (b * qb + i, 0)),
        out_shape=jax.ShapeDtypeStruct((batch * nq, BR), BF16),
        compiler_params=_cparams(2),
        name="diff_attention",
    )(*args)


def _da_pipelined_kernel(lam_ref, g_ref, q_ref, kl_ref, kc_ref, vl_ref, vc_ref, o_ref, e_ref, *, lam_init):
    n_lat = kl_ref.shape[0]

    @pl.when(pl.program_id(0) == 0)
    def _():
        e_ref[...] = jnp.ones(e_ref.shape, BF16)

    lam = lam_ref[0]
    outs = []
    for h in range(DA_HEADS):
        vs = slice(h * 128, (h + 1) * 128)
        att = []
        for j in range(2):
            u = 2 * h + j
            qs = slice(u * DA_DH, (u + 1) * DA_DH)
            acc = _dot(e_ref[u, :, :n_lat], vl_ref[:, vs]) + _dot(e_ref[u, :, n_lat:], vc_ref[:, vs])
            att.append((acc / pltpu.roll(acc, 64, 1))[:, :64])
            q = q_ref[:, qs]
            s_l = _dot_nt(q, kl_ref[:, qs])
            s_c = _dot_nt(q, kc_ref[:, qs])
            m = jnp.maximum(s_l.max(axis=-1, keepdims=True), s_c.max(axis=-1, keepdims=True))
            e_ref[u, :, :n_lat] = jnp.exp2(s_l - m).astype(BF16)
            e_ref[u, :, n_lat:] = jnp.exp2(s_c - m).astype(BF16)
        o = att[0] - lam * att[1]
        o = o * lax.rsqrt(jnp.mean(o * o, axis=-1, keepdims=True) + EPS) * g_ref[...] * (1.0 - lam_init)
        outs.append(o)
    o_ref[...] = jnp.concatenate(outs, axis=-1).astype(BF16)


def _diff_attention_pipelined(lam, subln, slab, slab_c, *, n, n_ctx, tq, lam_init):
    qb = n // tq
    steps = slab.shape[0] // tq

    def cur(s):
        return jnp.minimum(s, steps - 1)

    def prev(s):
        return jnp.maximum(s - 1, 0)

    return pl.pallas_call(
        functools.partial(_da_pipelined_kernel, lam_init=lam_init),
        grid=(steps + 1,),
        in_specs=[
            pl.BlockSpec(memory_space=pltpu.SMEM),
            _resident((1, DA_DV)),
            pl.BlockSpec((tq, BR), lambda s: (cur(s), COL_QA)),
            pl.BlockSpec((n, BR), lambda s: (cur(s) // qb, COL_KA)),
            pl.BlockSpec((n_ctx, BR), lambda s: (cur(s) // qb, COL_KA)),
            pl.BlockSpec((n, 2 * BR), lambda s: (prev(s) // qb, COL_VA)),
            pl.BlockSpec((n_ctx, 2 * BR), lambda s: (prev(s) // qb, COL_VA)),
        ],
        out_specs=pl.BlockSpec((tq, BR), lambda s: (prev(s), 0)),
        out_shape=jax.ShapeDtypeStruct((slab.shape[0], BR), BF16),
        scratch_shapes=[pltpu.VMEM((2 * DA_HEADS, tq, n + n_ctx), BF16)],
        compiler_params=_cparams(1),
        name="diff_attention_pipelined",
    )(lam, subln, slab, slab, slab_c, slab, slab_c)


def _na_bias_tables(rpb):
    rows = 2048 // GRID_W
    n_dc = 2 * NA_WIN_W - 1
    geoms = [(0, 0), (NA_QROWS, 0), (rows - NA_QROWS, rows - NA_KROWS)]
    qc = np.arange(GRID_W)[:, None]
    kc = np.arange(GRID_W)[None, :]
    c0 = np.clip(qc - NA_WIN_W // 2, 0, GRID_W - NA_WIN_W)
    col_ok = (kc >= c0) & (kc < c0 + NA_WIN_W)
    dc = np.clip(kc - qc, 1 - NA_WIN_W, NA_WIN_W - 1) + NA_WIN_W - 1
    col_onehot = (dc[None] == np.arange(n_dc)[:, None, None]).astype(np.float32)
    by_col = jnp.einsum("lhrd,dqc->lhrqc", rpb.astype(F32), col_onehot, precision=lax.Precision.HIGHEST)
    by_col = jnp.where(col_ok, by_col * LOG2E, NEG_INF)
    masked = jnp.full(by_col.shape[:2] + (GRID_W, GRID_W), NEG_INF, F32)
    tables = []
    for q0, ws in geoms:
        per_qrow = []
        for a in range(NA_QROWS):
            r = q0 + a
            r0 = min(max(r - NA_WIN_H // 2, 0), rows - NA_WIN_H)
            tiles = [by_col[:, :, ws + j - r + NA_WIN_H - 1] if r0 <= ws + j < r0 + NA_WIN_H else masked
                     for j in range(NA_KROWS)]
            per_qrow.append(jnp.concatenate(tiles, axis=-1))
        tables.append(jnp.concatenate(per_qrow, axis=2))
    return jnp.stack(tables, axis=1)


def _na_kernel(bias_ref, q_ref, kl_ref, vl_ref, kc_ref, vc_ref, o_ref):
    tq = NA_QROWS * GRID_W
    nkw = NA_KROWS * GRID_W
    n_units = q_ref.shape[0] // tq
    last_ws = q_ref.shape[0] // GRID_W - NA_KROWS
    def window(i):
        ws = min(max(NA_QROWS * i - NA_WIN_H // 2, 0), last_ws) * GRID_W
        return slice(ws, ws + nkw)

    def scores(i, h):
        hs = slice(h * NA_DH, (h + 1) * NA_DH)
        geom = 0 if i == 0 else (2 if i == n_units - 1 else 1)
        q = q_ref[i * tq:(i + 1) * tq, hs]
        return [_dot_nt(q, kl_ref[window(i), hs]) + bias_ref[geom, h], _dot_nt(q, kc_ref[:, hs])]

    units = [(i, h) for i in range(n_units) for h in range(NA_HEADS)]
    ready = scores(*units[0])
    outs = []
    for u, (i, h) in enumerate(units):
        upcoming = scores(*units[u + 1]) if u + 1 < len(units) else None
        vs = slice(h * 128, (h + 1) * 128)
        outs.append(_softmax_pv_scores(ready, [vl_ref[window(i), vs], vc_ref[:, vs]]))
        ready = upcoming
        if h == NA_HEADS - 1:
            o_ref[i * tq:(i + 1) * tq, :] = jnp.concatenate(outs, axis=-1).astype(BF16)
            outs = []


def _neigh_attention(bias_all, layer, slab, slab_c, *, n, n_ctx):
    batch = slab.shape[0] // n
    return pl.pallas_call(
        _na_kernel,
        grid=(batch,),
        in_specs=[
            pl.BlockSpec((None,) + bias_all.shape[1:], lambda b: (layer, 0, 0, 0, 0), pipeline_mode=pl.Buffered(1)),
            pl.BlockSpec((n, BR), lambda b: (b, COL_QB)),
            pl.BlockSpec((n, BR), lambda b: (b, COL_KB)),
            pl.BlockSpec((n, 2 * BR), lambda b: (b, COL_VB)),
            pl.BlockSpec((n_ctx, BR), lambda b: (b, COL_KB)),
            pl.BlockSpec((n_ctx, 2 * BR), lambda b: (b, COL_VB)),
        ],
        out_specs=pl.BlockSpec((n, BR), lambda b: (b, 0)),
        out_shape=jax.ShapeDtypeStruct((batch * n, BR), BF16),
        compiler_params=_cparams(1),
        name="neigh_attention",
    )(bias_all, slab, slab, slab, slab_c, slab_c)


def _ctx_attn_kernel(q_ref, k_ref, v_ref, o_ref):
    outs = []
    for h in range(NA_HEADS):
        hs = slice(h * NA_DH, (h + 1) * NA_DH)
        outs.append(_softmax_pv(q_ref[:, hs], [(k_ref[:, hs], v_ref[:, h * 128:(h + 1) * 128])]))
    o_ref[...] = jnp.concatenate(outs, axis=-1).astype(BF16)


def _ctx_attention(slab_c, *, n_ctx):
    batch = slab_c.shape[0] // n_ctx
    return pl.pallas_call(
        _ctx_attn_kernel,
        grid=(batch,),
        in_specs=[
            pl.BlockSpec((n_ctx, BR), lambda b: (b, COL_QB)),
            pl.BlockSpec((n_ctx, BR), lambda b: (b, COL_KB)),
            pl.BlockSpec((n_ctx, 2 * BR), lambda b: (b, COL_VB)),
        ],
        out_specs=pl.BlockSpec((n_ctx, BR), lambda b: (b, 0)),
        out_shape=jax.ShapeDtypeStruct((batch * n_ctx, BR), BF16),
        compiler_params=_cparams(1),
        name="ctx_attention",
    )(slab_c, slab_c, slab_c)


HG_PAIR = 2 * HG_D
HG_LEVELS = (32, 16, 8, 4, 2, 1)
HG_UNROLL = 4


def _hgrn_tables():
    c = HG_CHUNK
    t = np.arange(c)[:, None]
    s = np.arange(HG_PAIR)[None, :] % c
    cum = np.stack([np.arange(c)[None, :] <= t, np.arange(c)[None, :] >= t]).astype(np.float32)
    side = np.zeros((2, len(HG_LEVELS), c, HG_PAIR), np.float32)
    out = np.zeros((2, len(HG_LEVELS) + 1, c, HG_PAIR), np.float32)
    for d in range(2):
        out[d, 0] = t == s
        for li, m in enumerate(HG_LEVELS):
            q_t = (t % (2 * m) >= m) if d == 0 else (t % (2 * m) < m)
            k_s = (s % (2 * m) < m) if d == 0 else (s % (2 * m) >= m)
            side[d, li] = np.broadcast_to(q_t, (c, HG_PAIR))
            out[d, li + 1] = (t // (2 * m) == s // (2 * m)) & q_t & k_s
    pos = np.broadcast_to(t % 4, (c, HG_PAIR))
    coef = np.zeros((2, 3, c, HG_PAIR), np.float32)
    coef[0, 0] = pos >= 2
    coef[1, 0] = pos < 2
    coef[:, 1] = pos == 0
    coef[:, 2] = pos == 3
    r = np.arange(HG_PAIR)
    bd = (r[:, None] // HG_D == r[None, :] // HG_D).astype(np.float32)
    out = np.ascontiguousarray(out.transpose(0, 1, 3, 2))
    return (jnp.asarray(cum, BF16), jnp.asarray(side), jnp.asarray(out), jnp.asarray(coef),
            jnp.asarray(bd, BF16), jnp.asarray(bd))


def _row_block_broadcast(b, size, offset):
    parts = [jnp.broadcast_to(b[s + offset:s + offset + 1, :], (size, b.shape[1])) for s in range(0, b.shape[0], size)]
    return parts[0] if len(parts) == 1 else jnp.concatenate(parts, axis=0)


def _hgrn_unit(q16, v16, g, d, cum_ref, side_ref, out_ref, coef_ref, bd16_ref, bd_ref):
    c = HG_CHUNK
    bd16 = bd16_ref[...]

    def block_diag(x16):
        return jnp.concatenate([x16, x16], axis=0) * bd16

    q = q16.astype(F32)
    g_hi = g.astype(BF16)
    g_lo = (g - g_hi.astype(F32)).astype(BF16)
    cum = cum_ref[d]
    b = _dot(cum, g_hi) + _dot(cum, g_lo)
    f = jnp.exp(g)
    k = 1.0 - f
    yield
    btot = b[0:1, :] if d == 1 else b[c - 1:c, :]

    def level_scores(li, x):
        x16 = x.astype(BF16)
        return out_ref[d, li + 1] * _dot_nt(block_diag(x16), x16)

    att = out_ref[d, 0] * _dot_nt(block_diag(k.astype(BF16)), q16)
    for li, m in enumerate(HG_LEVELS):
        is_q = side_ref[d, li] != 0.0
        if m >= 4:
            ref_b = _row_block_broadcast(b, 2 * m, m if d == 1 else m - 1)
            decay = jnp.exp(-jnp.abs(b - ref_b))
            x = jnp.where(is_q, q, k) * decay
        elif m == 2:
            e = coef_ref[d, 0] * g + coef_ref[d, 1] * pltpu.roll(g, c - 1, 0) + coef_ref[d, 2] * pltpu.roll(g, 1, 0)
            x = jnp.where(is_q, q, k) * jnp.exp(e)
        else:
            x = jnp.where(is_q, q * f, k)
        att = att + level_scores(li, x)
        yield

    o_intra = _dot_tn(att.astype(BF16), block_diag(v16))
    q_in = (q * jnp.exp(b)).astype(BF16)
    yield
    upd = bd_ref[...] * _dot_tn(v16, (k * jnp.exp(btot - b)).astype(BF16))
    return o_intra, q_in, jnp.exp(btot), upd


def _hgrn_carry(st, o_intra, q_in, decay, upd):
    return o_intra + _dot_nt(q_in, st.astype(BF16)), st * decay + upd


def _interleave(units):
    results = [None] * len(units)
    live = list(range(len(units)))
    while live:
        for i in list(live):
            try:
                next(units[i])
            except StopIteration as done:
                results[i] = done.value
                live.remove(i)
    return results


def _hgrn_kernel(gain_ref, cum_ref, side_ref, out_ref, coef_ref, bd16_ref, bd_ref,
                 ql_ref, il_ref, gl_ref, qc_ref, ic_ref, gc_ref, yl_ref, yc_ref, st_ref, ol_ref, oc_ref):
    c = HG_CHUNK
    st_ref[...] = jnp.zeros(st_ref.shape, F32)

    def scan(q_ref, i_ref, g_ref, o_ref):
        n_chunks = q_ref.shape[0] // c
        pairs = [(d, p) for d in range(2) for p in range(BR // HG_PAIR)]

        def body(step, carry):
            where, units = [], []
            for u in range(HG_UNROLL):
                for d, p in pairs:
                    idx = step * HG_UNROLL + u
                    rows = pl.ds(pl.multiple_of((idx if d == 0 else n_chunks - 1 - idx) * c, c), c)
                    lanes = slice(p * HG_PAIR, (p + 1) * HG_PAIR)
                    g_lanes = slice(d * BR + p * HG_PAIR, d * BR + (p + 1) * HG_PAIR)
                    where.append((d, rows, lanes))
                    units.append(_hgrn_unit(q_ref[rows, lanes], i_ref[rows, lanes], g_ref[rows, g_lanes],
                                            d, cum_ref, side_ref, out_ref, coef_ref, bd16_ref, bd_ref))
            parts = _interleave(units)
            states = [st_ref[d, p] for d, p in pairs]
            for j, ((d, rows, lanes), part) in enumerate(zip(where, parts)):
                o, states[j % len(pairs)] = _hgrn_carry(states[j % len(pairs)], *part)
                o_ref[d, rows, lanes] = o
            for (d, p), st in zip(pairs, states):
                st_ref[d, p] = st
            return carry

        lax.fori_loop(0, n_chunks // HG_UNROLL, body, 0)

    scan(qc_ref, ic_ref, gc_ref, oc_ref)
    scan(ql_ref, il_ref, gl_ref, ol_ref)
    yl_ref[...] = _group_rmsnorm(ol_ref[0] + ol_ref[1], gain_ref[...], HG_D).astype(BF16)
    yc_ref[...] = _group_rmsnorm(oc_ref[0] + oc_ref[1], gain_ref[...], HG_D).astype(BF16)


def _hgrn(gain, tables, slab, g, slab_c, g_c, *, n, n_ctx):
    batch = slab.shape[0] // n
    return pl.pallas_call(
        _hgrn_kernel,
        grid=(batch,),
        in_specs=[
            _resident((1, BR)),
            *[_resident(t.shape) for t in tables],
            pl.BlockSpec((n, BR), lambda b: (b, COL_HQ)),
            pl.BlockSpec((n, BR), lambda b: (b, COL_HI)),
            pl.BlockSpec((n, 2 * BR), lambda b: (b, 0)),
            pl.BlockSpec((n_ctx, BR), lambda b: (b, COL_HQ)),
            pl.BlockSpec((n_ctx, BR), lambda b: (b, COL_HI)),
            pl.BlockSpec((n_ctx, 2 * BR), lambda b: (b, 0)),
        ],
        out_specs=[
            pl.BlockSpec((n, BR), lambda b: (b, 0)),
            pl.BlockSpec((n_ctx, BR), lambda b: (b, 0)),
        ],
        out_shape=[
            jax.ShapeDtypeStruct((batch * n, BR), BF16),
            jax.ShapeDtypeStruct((batch * n_ctx, BR), BF16),
        ],
        scratch_shapes=[
            pltpu.VMEM((2, BR // HG_PAIR, HG_PAIR, HG_PAIR), F32),
            pltpu.VMEM((2, n, BR), F32),
            pltpu.VMEM((2, n_ctx, BR), F32),
        ],
        compiler_params=_cparams(1),
        name="hgrn2",
    )(gain, *tables, slab, slab, g, slab_c, slab_c, g_c)


def _dft_tables(n):
    lo = 64
    hi = n // lo
    kk = np.arange(n, dtype=np.int64)
    ang_hi = 2.0 * np.pi * ((np.arange(hi, dtype=np.int64)[:, None] * lo * kk[None, :]) % n) / n
    ang_lo = 2.0 * np.pi * ((np.arange(lo, dtype=np.int64)[:, None] * kk[None, :]) % n) / n
    ch, sh = (jnp.asarray(f(ang_hi), F32)[:, None, :] for f in (np.cos, np.sin))
    cl, sl = (jnp.asarray(f(ang_lo), F32)[None, :, :] for f in (np.cos, np.sin))
    scale = 1.0 / math.sqrt(n)
    cos = ((ch * cl - sh * sl) * scale).reshape(n, n)
    sin = ((sh * cl + ch * sl) * scale).reshape(n, n)
    return jnp.concatenate([cos, -sin], axis=1).astype(BF16)


def _channel_dft_tables():
    j = np.arange(BR)
    same = (j[:, None] // FT_DG) == (j[None, :] // FT_DG)
    ang = 2.0 * np.pi * (((j[:, None] % FT_DG) * (j[None, :] % FT_DG)) % FT_DG) / FT_DG
    scale = 1.0 / math.sqrt(FT_DG)
    return (jnp.asarray(np.where(same, np.cos(ang), 0.0) * scale, F32).astype(BF16),
            jnp.asarray(np.where(same, np.sin(ang), 0.0) * scale, F32).astype(BF16))


def _fourier_kernel(cs_ref, bc_ref, bs_ref, u_ref, o_ref):
    u = u_ref[...]
    stacked = jnp.concatenate([_dot(u, bc_ref[...]).astype(BF16), _dot(u, bs_ref[...]).astype(BF16)], axis=0)
    o_ref[...] = _dot(cs_ref[...], stacked).astype(BF16)


def _fourier(cs, bc, bs, slab, *, n):
    batch = slab.shape[0] // n
    return pl.pallas_call(
        _fourier_kernel,
        grid=(batch,),
        in_specs=[
            _resident((n, 2 * n)),
            _resident((BR, BR)),
            _resident((BR, BR)),
            pl.BlockSpec((n, BR), lambda b: (b, COL_U)),
        ],
        out_specs=pl.BlockSpec((n, BR), lambda b: (b, 0)),
        out_shape=jax.ShapeDtypeStruct((batch * n, BR), BF16),
        compiler_params=_cparams(1),
        name="fourier",
    )(cs, bc, bs, slab)


def _merge_kernel(x_ref, mod_ref, ng_ref, ya_ref, yb_ref, yh_ref, yf_ref, sg_ref, wup_ref, wmg_ref, wout_ref, o_ref):
    x = x_ref[...]
    h = _modulated_norm(x, mod_ref, ng_ref).astype(BF16)
    acc = None
    for i, y_ref in enumerate((ya_ref, yb_ref, yh_ref, yf_ref)):
        y = (y_ref[...].astype(F32) * sg_ref[:, i * BR:(i + 1) * BR].astype(F32)).astype(BF16)
        term = jax.nn.sigmoid(_dot(h, wmg_ref[i])) * _dot(y, wup_ref[i])
        acc = term if acc is None else acc + term
    o_ref[...] = x + mod_ref[2:3, :] * _dot(acc.astype(BF16), wout_ref[...])


def _merge(x2d, mod_l, ng, ys, slab, wup, wmg, wout, *, tm, rows_per_mod, mod_row0):
    n = x2d.shape[0]

    def mod_map(t):
        return ((t * tm) // rows_per_mod + mod_row0, 0, 0)

    return pl.pallas_call(
        _merge_kernel,
        grid=(n // tm,),
        in_specs=[
            pl.BlockSpec((tm, D_MODEL), lambda t: (t, 0)),
            pl.BlockSpec((None, 3, D_MODEL), mod_map),
            _resident((1, D_MODEL)),
            *[pl.BlockSpec((tm, BR), lambda t: (t, 0)) for _ in range(4)],
            pl.BlockSpec((tm, 4 * BR), lambda t: (t, COL_SG)),
            _resident((4, BR, D_MODEL)),
            _resident((4, D_MODEL, D_MODEL)),
            _resident((D_MODEL, D_MODEL)),
        ],
        out_specs=pl.BlockSpec((tm, D_MODEL), lambda t: (t, 0)),
        out_shape=jax.ShapeDtypeStruct((n, D_MODEL), F32),
        compiler_params=_cparams(1),
        name="merge",
    )(x2d, mod_l, ng, *ys, slab, wup, wmg, wout)


def _rope_tables(n):
    t = jnp.arange(n)
    row = (t // GRID_W).astype(F32)
    col = (t % GRID_W).astype(F32)
    d_axis = DA_DH // 2
    inv = ROPE_THETA ** (-jnp.arange(0, d_axis, 2, dtype=F32) / d_axis)
    ang = jnp.concatenate([row[:, None] * inv, col[:, None] * inv], axis=-1)
    cos, sin = jnp.cos(ang), jnp.sin(ang)
    cos_h = jnp.concatenate([cos, cos], axis=-1)
    sin_h = jnp.concatenate([-sin, sin], axis=-1)
    return jnp.tile(cos_h, (1, 2 * DA_HEADS)), jnp.tile(sin_h, (1, 2 * DA_HEADS))


def kernel(x, c, ctx, c_ctx, norm_gain, w_mod, b_mod, w_in, da_qk_gain, da_lambda, da_subln_gain,
           na_qk_gain, na_rpb, hg_lb_logits, hg_norm_gain, w_up, w_merge, w_out):
    batch, n, _ = x.shape
    n_ctx = ctx.shape[1]
    assert n == 2048 and n % GRID_W == 0 and batch + 1 <= MOD_ROWS
    assert batch % 2 == 0

    cc = jnp.concatenate([c, c_ctx[None, :], jnp.zeros((MOD_ROWS - batch - 1, D_MODEL), F32)], axis=0)
    mod = _modulation(cc, w_mod, b_mod).reshape(DEPTH, MOD_ROWS, 3, D_MODEL)

    p_lb = jax.nn.softmax(hg_lb_logits.astype(F32), axis=1)
    lower = jnp.cumsum(p_lb, axis=1) - p_lb[:, :1]
    lb_floored = jnp.maximum(lower, LB_FLOOR)
    one_minus_lb = 1.0 - lower
    lv = da_lambda.astype(F32)
    lam_all = jnp.exp(jnp.sum(lv[:, 0] * lv[:, 1], axis=-1)) - jnp.exp(jnp.sum(lv[:, 2] * lv[:, 3], axis=-1))

    cos, sin = _rope_tables(n)
    cs_lat, cs_ctx = _dft_tables(n), _dft_tables(n_ctx)
    bc, bs = _channel_dft_tables()
    bias_all = _na_bias_tables(na_rpb)
    hg_tables = _hgrn_tables()

    xl = x.reshape(batch * n, D_MODEL)
    xc = ctx.reshape(batch * n_ctx, D_MODEL)
    for l in range(DEPTH):
        need_ctx = l < DEPTH - 1
        lam_init = 0.8 - 0.6 * math.exp(-0.3 * l)
        lam = (lam_all[l] + lam_init).reshape(1)
        ng = norm_gain[l].reshape(1, D_MODEL)
        w = w_in[l].astype(BF16)
        gda = jnp.tile(da_qk_gain[l], (1, BR // DA_DH)) * jnp.array([[DA_DH ** -0.5 * LOG2E], [1.0]], F32)
        gna = jnp.tile(na_qk_gain[l], (1, BR // NA_DH)) * jnp.array([[NA_DH ** -0.5 * LOG2E], [1.0]], F32)
        lbp = jnp.stack([lb_floored[0, l], one_minus_lb[0, l], lb_floored[1, l], one_minus_lb[1, l]])
        subln = da_subln_gain[l].reshape(1, DA_DV)
        hgain = jnp.tile(hg_norm_gain[l].reshape(1, HG_D), (1, HG_HEADS))

        slab, g = _inproj(xl, mod[l], ng, w, gda, gna, lbp, cos, sin,
                          tm=512, rows_per_mod=n, mod_row0=0, rope=True)
        slab_c, g_c = _inproj(xc, mod[l], ng, w, gda, gna, lbp, cos[:2 * n_ctx], sin[:2 * n_ctx],
                              tm=2 * n_ctx, rows_per_mod=batch * n_ctx, mod_row0=batch, rope=False)

        ya = _diff_attention_pipelined(lam, subln, slab, slab_c, n=n, n_ctx=n_ctx, tq=DA_QUNIT, lam_init=lam_init)
        yb = _neigh_attention(bias_all, l, slab, slab_c, n=n, n_ctx=n_ctx)
        yh, yh_c = _hgrn(hgain, hg_tables, slab, g, slab_c, g_c, n=n, n_ctx=n_ctx)
        yf = _fourier(cs_lat, bc, bs, slab, n=n)

        wup = w_up[l].astype(BF16)
        wmg = w_merge[l].astype(BF16)
        wout = w_out[l].astype(BF16)
        xl = _merge(xl, mod[l], ng, (ya, yb, yh, yf), slab, wup, wmg, wout,
                    tm=512, rows_per_mod=n, mod_row0=0)
        if need_ctx:
            ya_c = _diff_attention(lam, subln, slab_c, [(slab_c, n_ctx)], nq=n_ctx, tq=n_ctx, lam_init=lam_init)
            yb_c = _ctx_attention(slab_c, n_ctx=n_ctx)
            yf_c = _fourier(cs_ctx, bc, bs, slab_c, n=n_ctx)
            xc = _merge(xc, mod[l], ng, (ya_c, yb_c, yh_c, yf_c), slab_c, wup, wmg, wout,
                        tm=n_ctx, rows_per_mod=batch * n_ctx, mod_row0=batch)
    return xl.reshape(batch, n, D_MODEL)
```

```python
import functools
import math

import numpy as np
import jax
import jax.numpy as jnp
from jax import lax
from jax.experimental import pallas as pl
from jax.experimental.pallas import tpu as pltpu

D_MODEL = 1024
DEPTH = 4
GRID_W = 64
BR = 256
DA_HEADS, DA_DH, DA_DV = 4, 32, 64
NA_HEADS, NA_DH, NA_WIN_H, NA_WIN_W = 4, 64, 8, 16
HG_HEADS, HG_D, HG_CHUNK = 4, 64, 64
FT_DG = 64
IN_WIDTH = 15 * BR
LB_FLOOR = 1e-20
ROPE_THETA = 10000.0
EPS = 1e-6
NEG_INF = -1e30

COL_SG = 0
COL_QA, COL_KA, COL_QB, COL_KB = 4, 5, 6, 7
COL_VA, COL_VB = 4, 5
COL_HQ, COL_HI, COL_U = 12, 13, 14
SLAB_W = 15 * BR
LOG2E = 1.4426950408889634

DA_QUNIT = 512
NA_QROWS = 4
NA_KROWS = 12
MOD_ROWS = 24

VMEM_LIMIT_BYTES = 56 * 1024 * 1024

F32 = jnp.float32
BF16 = jnp.bfloat16


def _cparams(n_axes):
    return pltpu.CompilerParams(dimension_semantics=("arbitrary",) * n_axes, vmem_limit_bytes=VMEM_LIMIT_BYTES)


def _resident(shape):
    nd = len(shape)
    return pl.BlockSpec(shape, lambda *_: (0,) * nd, pipeline_mode=pl.Buffered(1))


def _dot(a, b):
    return jnp.dot(a, b, preferred_element_type=F32)


def _dot_nt(a, b):
    return lax.dot_general(a, b, (((1,), (1,)), ((), ())), preferred_element_type=F32)


def _dot_tn(a, b):
    return lax.dot_general(a, b, (((0,), (0,)), ((), ())), preferred_element_type=F32)


def _split3(v):
    hi = v.astype(BF16)
    r = v - hi.astype(F32)
    mid = r.astype(BF16)
    lo = (r - mid.astype(F32)).astype(BF16)
    return hi, mid, lo


def _group_mean(v, gsize, passes):
    w = v.shape[-1]
    r = lax.broadcasted_iota(jnp.int32, (w, w), 0) // gsize
    c = lax.broadcasted_iota(jnp.int32, (w, w), 1) // gsize
    ones = jnp.where(r == c, 1.0 / gsize, 0.0).astype(BF16)
    total, rest = None, v
    for _ in range(passes):
        piece = rest.astype(BF16)
        rest = rest - piece.astype(F32)
        part = _dot(piece, ones)
        total = part if total is None else total + part
    return total


def _group_rmsnorm(v, gain, gsize, passes=2):
    return v * lax.rsqrt(_group_mean(v * v, gsize, passes) + EPS) * gain


def _silu(v):
    return v * jax.nn.sigmoid(v)


def _mod_kernel(c_ref, w_ref, b_ref, o_ref):
    s = _silu(c_ref[...]).astype(BF16)
    o_ref[...] = _dot(s, w_ref[...].astype(BF16)) + b_ref[...]


def _modulation(cc, w_mod, b_mod):
    tn = D_MODEL
    return pl.pallas_call(
        _mod_kernel,
        grid=(DEPTH, 3 * D_MODEL // tn),
        in_specs=[
            pl.BlockSpec((MOD_ROWS, D_MODEL), lambda l, j: (0, 0)),
            pl.BlockSpec((None, D_MODEL, tn), lambda l, j: (l, 0, j)),
            pl.BlockSpec((None, 1, tn), lambda l, j: (l, 0, j)),
        ],
        out_specs=pl.BlockSpec((None, MOD_ROWS, tn), lambda l, j: (l, 0, j)),
        out_shape=jax.ShapeDtypeStruct((DEPTH, MOD_ROWS, 3 * D_MODEL), F32),
        compiler_params=_cparams(2),
        name="modulation",
    )(cc, w_mod, b_mod.reshape(DEPTH, 1, 3 * D_MODEL))


def _modulated_norm(x, mod_ref, ng_ref):
    ms = jnp.mean(x * x, axis=-1, keepdims=True)
    row_scale = ng_ref[...] * (1.0 + mod_ref[1:2, :])
    return x * lax.rsqrt(ms + EPS) * row_scale + mod_ref[0:1, :]


def _log_forget(f, lb_floored, one_minus_lb):
    return jnp.log(lb_floored + one_minus_lb * jax.nn.sigmoid(f))


def _rope(v, cos, sin_signed):
    lane = lax.broadcasted_iota(jnp.int32, v.shape, 1)
    first_half = (lane % DA_DH) < (DA_DH // 2)
    swapped = jnp.where(first_half, pltpu.roll(v, BR - DA_DH // 2, 1), pltpu.roll(v, DA_DH // 2, 1))
    return v * cos + swapped * sin_signed


def _inproj_kernel(x_ref, mod_ref, ng_ref, w_ref, gda_ref, gna_ref, lb_ref, cos_ref, sin_ref, slab_ref, g_ref, *, rope):
    h = _modulated_norm(x_ref[...], mod_ref, ng_ref).astype(BF16)

    def proj(col):
        return _dot(h, w_ref[:, col * BR:(col + 1) * BR])

    def put(col, v):
        slab_ref[:, col * BR:(col + 1) * BR] = v.astype(BF16)

    def put_values(col2, v):
        ones = jnp.ones((v.shape[0], 64), F32)
        pieces = []
        for hd in range(BR // 64):
            pieces += [v[:, hd * 64:(hd + 1) * 64], ones]
        slab_ref[:, col2 * 2 * BR:(col2 + 1) * 2 * BR] = jnp.concatenate(pieces, axis=-1).astype(BF16)

    raw_qa, raw_ka, raw_qb, raw_kb = proj(0), proj(1), proj(3), proj(4)
    g_ref[:, 0:BR] = _log_forget(proj(7), lb_ref[0:1, :], lb_ref[1:2, :])
    g_ref[:, BR:2 * BR] = _log_forget(proj(8), lb_ref[2:3, :], lb_ref[3:4, :])
    q = _group_rmsnorm(raw_qa, gda_ref[0:1, :], DA_DH, passes=1)
    k = _group_rmsnorm(raw_ka, gda_ref[1:2, :], DA_DH, passes=1)
    if rope:
        q = _rope(q, cos_ref[...], sin_ref[...])
        k = _rope(k, cos_ref[...], sin_ref[...])
    put(COL_QA, q)
    put(COL_KA, k)
    put(COL_QB, _group_rmsnorm(raw_qb, gna_ref[0:1, :], NA_DH, passes=1))
    put(COL_KB, _group_rmsnorm(raw_kb, gna_ref[1:2, :], NA_DH, passes=1))
    for i in range(4):
        put(COL_SG + i, _silu(proj(11 + i)))
    put(COL_HQ, _silu(proj(6)))
    put_values(COL_VA, proj(2))
    put_values(COL_VB, proj(5))
    put(COL_HI, proj(9))
    put(COL_U, proj(10))


def _inproj(x2d, mod_l, ng, w, gda, gna, lbp, cos, sin, *, tm, rows_per_mod, mod_row0, rope):
    n = x2d.shape[0]
    n_pos = cos.shape[0]
    tiles_per_seq = n_pos // tm

    def mod_map(t):
        return ((t * tm) // rows_per_mod + mod_row0, 0, 0)

    return pl.pallas_call(
        functools.partial(_inproj_kernel, rope=rope),
        grid=(n // tm,),
        in_specs=[
            pl.BlockSpec((tm, D_MODEL), lambda t: (t, 0)),
            pl.BlockSpec((None, 3, D_MODEL), mod_map),
            _resident((1, D_MODEL)),
            _resident((D_MODEL, IN_WIDTH)),
            _resident((2, BR)),
            _resident((2, BR)),
            _resident((4, BR)),
            pl.BlockSpec((tm, BR), lambda t: (t % tiles_per_seq, 0)),
            pl.BlockSpec((tm, BR), lambda t: (t % tiles_per_seq, 0)),
        ],
        out_specs=[
            pl.BlockSpec((tm, SLAB_W), lambda t: (t, 0)),
            pl.BlockSpec((tm, 2 * BR), lambda t: (t, 0)),
        ],
        out_shape=[
            jax.ShapeDtypeStruct((n, SLAB_W), BF16),
            jax.ShapeDtypeStruct((n, 2 * BR), F32),
        ],
        compiler_params=_cparams(1),
        name="inproj_rope" if rope else "inproj",
    )(x2d, mod_l, ng, w, gda, gna, lbp, cos, sin)


def _softmax_pv_scores(scores, values):
    m = scores[0].max(axis=-1, keepdims=True)
    for s in scores[1:]:
        m = jnp.maximum(m, s.max(axis=-1, keepdims=True))
    acc = None
    for s, v in zip(scores, values):
        p = _dot(jnp.exp2(s - m).astype(BF16), v)
        acc = p if acc is None else acc + p
    return (acc / pltpu.roll(acc, 64, 1))[:, :64]


def _softmax_pv(q, segments):
    return _softmax_pv_scores([_dot_nt(q, k) for k, _ in segments], [v for _, v in segments])


def _da_kernel(lam_ref, g_ref, q_ref, *refs, lam_init, n_seg):
    kv_refs, o_ref = refs[:2 * n_seg], refs[2 * n_seg]
    lam = lam_ref[0]
    for r0 in range(0, q_ref.shape[0], DA_QUNIT):
        rows = slice(r0, min(r0 + DA_QUNIT, q_ref.shape[0]))
        outs = []
        for h in range(DA_HEADS):
            vs = slice(h * 128, (h + 1) * 128)
            att = []
            for j in range(2):
                qs = slice((2 * h + j) * DA_DH, (2 * h + j + 1) * DA_DH)
                segs = [(kv_refs[2 * s][:, qs], kv_refs[2 * s + 1][:, vs]) for s in range(n_seg)]
                att.append(_softmax_pv(q_ref[rows, qs], segs))
            o = att[0] - lam * att[1]
            o = o * lax.rsqrt(jnp.mean(o * o, axis=-1, keepdims=True) + EPS) * g_ref[...] * (1.0 - lam_init)
            outs.append(o)
        o_ref[rows, :] = jnp.concatenate(outs, axis=-1).astype(BF16)


def _diff_attention(lam, subln, slab_q, key_slabs, *, nq, tq, lam_init):
    batch = slab_q.shape[0] // nq
    qb = nq // tq
    in_specs = [
        pl.BlockSpec(memory_space=pltpu.SMEM),
        _resident((1, DA_DV)),
        pl.BlockSpec((tq, BR), lambda b, i: (b * qb + i, COL_QA)),
    ]
    args = [lam, subln, slab_q]
    for slab, nk in key_slabs:
        in_specs += [pl.BlockSpec((nk, BR), lambda b, i: (b, COL_KA)),
                     pl.BlockSpec((nk, 2 * BR), lambda b, i: (b, COL_VA))]
        args += [slab, slab]
    return pl.pallas_call(
        functools.partial(_da_kernel, lam_init=lam_init, n_seg=len(key_slabs)),
        grid=(batch, qb),
        in_specs=in_specs,
        out_specs=pl.BlockSpec((tq, BR), lambda b, i: (b * qb + i, 0)),
        out_shape=jax.ShapeDtypeStruct((batch * nq, BR), BF16),
        compiler_params=_cparams(2),
        name="diff_attention",
    )(*args)


def _da_pipelined_kernel(lam_ref, g_ref, q_ref, kl_ref, kc_ref, vl_ref, vc_ref, o_ref, e_ref, *, lam_init):
    n_lat = kl_ref.shape[0]

    @pl.when(pl.program_id(0) == 0)
    def _():
        e_ref[...] = jnp.ones(e_ref.shape, BF16)

    lam = lam_ref[0]
    outs = []
    for h in range(DA_HEADS):
        vs = slice(h * 128, (h + 1) * 128)
        att = []
        for j in range(2):
            u = 2 * h + j
            qs = slice(u * DA_DH, (u + 1) * DA_DH)
            acc = _dot(e_ref[u, :, :n_lat], vl_ref[:, vs]) + _dot(e_ref[u, :, n_lat:], vc_ref[:, vs])
            att.append((acc / pltpu.roll(acc, 64, 1))[:, :64])
            q = q_ref[:, qs]
            s_l = _dot_nt(q, kl_ref[:, qs])
            s_c = _dot_nt(q, kc_ref[:, qs])
            m = jnp.maximum(s_l.max(axis=-1, keepdims=True), s_c.max(axis=-1, keepdims=True))
            e_ref[u, :, :n_lat] = jnp.exp2((s_l - m).astype(BF16))
            e_ref[u, :, n_lat:] = jnp.exp2((s_c - m).astype(BF16))
        o = att[0] - lam * att[1]
        o = o * lax.rsqrt(jnp.mean(o * o, axis=-1, keepdims=True) + EPS) * g_ref[...] * (1.0 - lam_init)
        outs.append(o)
    o_ref[...] = jnp.concatenate(outs, axis=-1).astype(BF16)


def _diff_attention_pipelined(lam, subln, slab, slab_c, *, n, n_ctx, tq, lam_init):
    qb = n // tq
    steps = slab.shape[0] // tq

    def cur(s):
        return jnp.minimum(s, steps - 1)

    def prev(s):
        return jnp.maximum(s - 1, 0)

    return pl.pallas_call(
        functools.partial(_da_pipelined_kernel, lam_init=lam_init),
        grid=(steps + 1,),
        in_specs=[
            pl.BlockSpec(memory_space=pltpu.SMEM),
            _resident((1, DA_DV)),
            pl.BlockSpec((tq, BR), lambda s: (cur(s), COL_QA)),
            pl.BlockSpec((n, BR), lambda s: (cur(s) // qb, COL_KA)),
            pl.BlockSpec((n_ctx, BR), lambda s: (cur(s) // qb, COL_KA)),
            pl.BlockSpec((n, 2 * BR), lambda s: (prev(s) // qb, COL_VA)),
            pl.BlockSpec((n_ctx, 2 * BR), lambda s: (prev(s) // qb, COL_VA)),
        ],
        out_specs=pl.BlockSpec((tq, BR), lambda s: (prev(s), 0)),
        out_shape=jax.ShapeDtypeStruct((slab.shape[0], BR), BF16),
        scratch_shapes=[pltpu.VMEM((2 * DA_HEADS, tq, n + n_ctx), BF16)],
        compiler_params=_cparams(1),
        name="diff_attention_pipelined",
    )(lam, subln, slab, slab, slab_c, slab, slab_c)


def _na_bias_tables(rpb):
    rows = 2048 // GRID_W
    n_dc = 2 * NA_WIN_W - 1
    geoms = [(0, 0), (NA_QROWS, 0), (rows - NA_QROWS, rows - NA_KROWS)]
    qc = np.arange(GRID_W)[:, None]
    kc = np.arange(GRID_W)[None, :]
    c0 = np.clip(qc - NA_WIN_W // 2, 0, GRID_W - NA_WIN_W)
    col_ok = (kc >= c0) & (kc < c0 + NA_WIN_W)
    dc = np.clip(kc - qc, 1 - NA_WIN_W, NA_WIN_W - 1) + NA_WIN_W - 1
    col_onehot = (dc[None] == np.arange(n_dc)[:, None, None]).astype(np.float32)
    by_col = jnp.einsum("lhrd,dqc->lhrqc", rpb.astype(F32), col_onehot, precision=lax.Precision.HIGHEST)
    by_col = jnp.where(col_ok, by_col * LOG2E, NEG_INF)
    masked = jnp.full(by_col.shape[:2] + (GRID_W, GRID_W), NEG_INF, F32)
    tables = []
    for q0, ws in geoms:
        per_qrow = []
        for a in range(NA_QROWS):
            r = q0 + a
            r0 = min(max(r - NA_WIN_H // 2, 0), rows - NA_WIN_H)
            tiles = [by_col[:, :, ws + j - r + NA_WIN_H - 1] if r0 <= ws + j < r0 + NA_WIN_H else masked
                     for j in range(NA_KROWS)]
            per_qrow.append(jnp.concatenate(tiles, axis=-1))
        tables.append(jnp.concatenate(per_qrow, axis=2))
    return jnp.stack(tables, axis=1)


def _na_kernel(bias_ref, q_ref, kl_ref, vl_ref, kc_ref, vc_ref, o_ref):
    tq = NA_QROWS * GRID_W
    nkw = NA_KROWS * GRID_W
    n_units = q_ref.shape[0] // tq
    last_ws = q_ref.shape[0] // GRID_W - NA_KROWS
    def window(i):
        ws = min(max(NA_QROWS * i - NA_WIN_H // 2, 0), last_ws) * GRID_W
        return slice(ws, ws + nkw)

    def scores(i, h):
        hs = slice(h * NA_DH, (h + 1) * NA_DH)
        geom = 0 if i == 0 else (2 if i == n_units - 1 else 1)
        q = q_ref[i * tq:(i + 1) * tq, hs]
        return [_dot_nt(q, kl_ref[window(i), hs]) + bias_ref[geom, h], _dot_nt(q, kc_ref[:, hs])]

    units = [(i, h) for i in range(n_units) for h in range(NA_HEADS)]
    ready = scores(*units[0])
    outs = []
    for u, (i, h) in enumerate(units):
        upcoming = scores(*units[u + 1]) if u + 1 < len(units) else None
        vs = slice(h * 128, (h + 1) * 128)
        outs.append(_softmax_pv_scores(ready, [vl_ref[window(i), vs], vc_ref[:, vs]]))
        ready = upcoming
        if h == NA_HEADS - 1:
            o_ref[i * tq:(i + 1) * tq, :] = jnp.concatenate(outs, axis=-1).astype(BF16)
            outs = []


def _neigh_attention(bias_all, layer, slab, slab_c, *, n, n_ctx):
    batch = slab.shape[0] // n
    return pl.pallas_call(
        _na_kernel,
        grid=(batch,),
        in_specs=[
            pl.BlockSpec((None,) + bias_all.shape[1:], lambda b: (layer, 0, 0, 0, 0), pipeline_mode=pl.Buffered(1)),
            pl.BlockSpec((n, BR), lambda b: (b, COL_QB)),
            pl.BlockSpec((n, BR), lambda b: (b, COL_KB)),
            pl.BlockSpec((n, 2 * BR), lambda b: (b, COL_VB)),
            pl.BlockSpec((n_ctx, BR), lambda b: (b, COL_KB)),
            pl.BlockSpec((n_ctx, 2 * BR), lambda b: (b, COL_VB)),
        ],
        out_specs=pl.BlockSpec((n, BR), lambda b: (b, 0)),
        out_shape=jax.ShapeDtypeStruct((batch * n, BR), BF16),
        compiler_params=_cparams(1),
        name="neigh_attention",
    )(bias_all, slab, slab, slab, slab_c, slab_c)


def _ctx_attn_kernel(q_ref, k_ref, v_ref, o_ref):
    outs = []
    for h in range(NA_HEADS):
        hs = slice(h * NA_DH, (h + 1) * NA_DH)
        outs.append(_softmax_pv(q_ref[:, hs], [(k_ref[:, hs], v_ref[:, h * 128:(h + 1) * 128])]))
    o_ref[...] = jnp.concatenate(outs, axis=-1).astype(BF16)


def _ctx_attention(slab_c, *, n_ctx):
    batch = slab_c.shape[0] // n_ctx
    return pl.pallas_call(
        _ctx_attn_kernel,
        grid=(batch,),
        in_specs=[
            pl.BlockSpec((n_ctx, BR), lambda b: (b, COL_QB)),
            pl.BlockSpec((n_ctx, BR), lambda b: (b, COL_KB)),
            pl.BlockSpec((n_ctx, 2 * BR), lambda b: (b, COL_VB)),
        ],
        out_specs=pl.BlockSpec((n_ctx, BR), lambda b: (b, 0)),
        out_shape=jax.ShapeDtypeStruct((batch * n_ctx, BR), BF16),
        compiler_params=_cparams(1),
        name="ctx_attention",
    )(slab_c, slab_c, slab_c)


HG_PAIR = 2 * HG_D
HG_LEVELS = (32, 16, 8, 4, 2, 1)
HG_UNROLL = 4


def _hgrn_tables():
    c = HG_CHUNK
    t = np.arange(c)[:, None]
    s = np.arange(HG_PAIR)[None, :] % c
    cum = np.stack([np.arange(c)[None, :] <= t, np.arange(c)[None, :] >= t]).astype(np.float32)
    side = np.zeros((2, len(HG_LEVELS), c, HG_PAIR), np.float32)
    out = np.zeros((2, len(HG_LEVELS) + 1, c, HG_PAIR), np.float32)
    for d in range(2):
        out[d, 0] = t == s
        for li, m in enumerate(HG_LEVELS):
            q_t = (t % (2 * m) >= m) if d == 0 else (t % (2 * m) < m)
            k_s = (s % (2 * m) < m) if d == 0 else (s % (2 * m) >= m)
            side[d, li] = np.broadcast_to(q_t, (c, HG_PAIR))
            out[d, li + 1] = (t // (2 * m) == s // (2 * m)) & q_t & k_s
    pos = np.broadcast_to(t % 4, (c, HG_PAIR))
    coef = np.zeros((2, 3, c, HG_PAIR), np.float32)
    coef[0, 0] = pos >= 2
    coef[1, 0] = pos < 2
    coef[:, 1] = pos == 0
    coef[:, 2] = pos == 3
    r = np.arange(HG_PAIR)
    bd = (r[:, None] // HG_D == r[None, :] // HG_D).astype(np.float32)
    out = np.ascontiguousarray(out.transpose(0, 1, 3, 2))
    return (jnp.asarray(cum, BF16), jnp.asarray(side), jnp.asarray(out), jnp.asarray(coef),
            jnp.asarray(bd, BF16), jnp.asarray(bd))


def _row_block_broadcast(b, size, offset):
    parts = [jnp.broadcast_to(b[s + offset:s + offset + 1, :], (size, b.shape[1])) for s in range(0, b.shape[0], size)]
    return parts[0] if len(parts) == 1 else jnp.concatenate(parts, axis=0)


def _hgrn_unit(q16, v16, g, d, cum_ref, side_ref, out_ref, coef_ref, bd16_ref, bd_ref):
    c = HG_CHUNK
    bd16 = bd16_ref[...]

    def block_diag(x16):
        return jnp.concatenate([x16, x16], axis=0) * bd16

    q = q16.astype(F32)
    g_hi = g.astype(BF16)
    g_lo = (g - g_hi.astype(F32)).astype(BF16)
    cum = cum_ref[d]
    b = _dot(cum, g_hi) + _dot(cum, g_lo)
    f = jnp.exp(g)
    k = 1.0 - f
    yield
    btot = b[0:1, :] if d == 1 else b[c - 1:c, :]

    def level_scores(li, x):
        x16 = x.astype(BF16)
        return out_ref[d, li + 1] * _dot_nt(block_diag(x16), x16)

    att = out_ref[d, 0] * _dot_nt(block_diag(k.astype(BF16)), q16)
    for li, m in enumerate(HG_LEVELS):
        is_q = side_ref[d, li] != 0.0
        if m >= 4:
            ref_b = _row_block_broadcast(b, 2 * m, m if d == 1 else m - 1)
            decay = jnp.exp(-jnp.abs(b - ref_b))
            x = jnp.where(is_q, q, k) * decay
        elif m == 2:
            e = coef_ref[d, 0] * g + coef_ref[d, 1] * pltpu.roll(g, c - 1, 0) + coef_ref[d, 2] * pltpu.roll(g, 1, 0)
            x = jnp.where(is_q, q, k) * jnp.exp(e)
        else:
            x = jnp.where(is_q, q * f, k)
        att = att + level_scores(li, x)
        yield

    o_intra = _dot_tn(att.astype(BF16), block_diag(v16))
    q_in = (q * jnp.exp(b)).astype(BF16)
    yield
    upd = bd_ref[...] * _dot_tn(v16, (k * jnp.exp(btot - b)).astype(BF16))
    return o_intra, q_in, jnp.exp(btot), upd


def _hgrn_carry(st, o_intra, q_in, decay, upd):
    return o_intra + _dot_nt(q_in, st.astype(BF16)), st * decay + upd


def _interleave(units):
    results = [None] * len(units)
    live = list(range(len(units)))
    while live:
        for i in list(live):
            try:
                next(units[i])
            except StopIteration as done:
                results[i] = done.value
                live.remove(i)
    return results


def _hgrn_kernel(gain_ref, cum_ref, side_ref, out_ref, coef_ref, bd16_ref, bd_ref,
                 ql_ref, il_ref, gl_ref, qc_ref, ic_ref, gc_ref, yl_ref, yc_ref, st_ref, ol_ref, oc_ref):
    c = HG_CHUNK
    st_ref[...] = jnp.zeros(st_ref.shape, F32)

    def scan(q_ref, i_ref, g_ref, o_ref):
        n_chunks = q_ref.shape[0] // c
        pairs = [(d, p) for d in range(2) for p in range(BR // HG_PAIR)]

        def body(step, carry):
            where, units = [], []
            for u in range(HG_UNROLL):
                for d, p in pairs:
                    idx = step * HG_UNROLL + u
                    rows = pl.ds(pl.multiple_of((idx if d == 0 else n_chunks - 1 - idx) * c, c), c)
                    lanes = slice(p * HG_PAIR, (p + 1) * HG_PAIR)
                    g_lanes = slice(d * BR + p * HG_PAIR, d * BR + (p + 1) * HG_PAIR)
                    where.append((d, rows, lanes))
                    units.append(_hgrn_unit(q_ref[rows, lanes], i_ref[rows, lanes], g_ref[rows, g_lanes],
                                            d, cum_ref, side_ref, out_ref, coef_ref, bd16_ref, bd_ref))
            parts = _interleave(units)
            states = [st_ref[d, p] for d, p in pairs]
            for j, ((d, rows, lanes), part) in enumerate(zip(where, parts)):
                o, states[j % len(pairs)] = _hgrn_carry(states[j % len(pairs)], *part)
                o_ref[d, rows, lanes] = o
            for (d, p), st in zip(pairs, states):
                st_ref[d, p] = st
            return carry

        lax.fori_loop(0, n_chunks // HG_UNROLL, body, 0)

    scan(qc_ref, ic_ref, gc_ref, oc_ref)
    scan(ql_ref, il_ref, gl_ref, ol_ref)
    yl_ref[...] = _group_rmsnorm(ol_ref[0] + ol_ref[1], gain_ref[...], HG_D).astype(BF16)
    yc_ref[...] = _group_rmsnorm(oc_ref[0] + oc_ref[1], gain_ref[...], HG_D).astype(BF16)


def _hgrn(gain, tables, slab, g, slab_c, g_c, *, n, n_ctx):
    batch = slab.shape[0] // n
    return pl.pallas_call(
        _hgrn_kernel,
        grid=(batch,),
        in_specs=[
            _resident((1, BR)),
            *[_resident(t.shape) for t in tables],
            pl.BlockSpec((n, BR), lambda b: (b, COL_HQ)),
            pl.BlockSpec((n, BR), lambda b: (b, COL_HI)),
            pl.BlockSpec((n, 2 * BR), lambda b: (b, 0)),
            pl.BlockSpec((n_ctx, BR), lambda b: (b, COL_HQ)),
            pl.BlockSpec((n_ctx, BR), lambda b: (b, COL_HI)),
            pl.BlockSpec((n_ctx, 2 * BR), lambda b: (b, 0)),
        ],
        out_specs=[
            pl.BlockSpec((n, BR), lambda b: (b, 0)),
            pl.BlockSpec((n_ctx, BR), lambda b: (b, 0)),
        ],
        out_shape=[
            jax.ShapeDtypeStruct((batch * n, BR), BF16),
            jax.ShapeDtypeStruct((batch * n_ctx, BR), BF16),
        ],
        scratch_shapes=[
            pltpu.VMEM((2, BR // HG_PAIR, HG_PAIR, HG_PAIR), F32),
            pltpu.VMEM((2, n, BR), F32),
            pltpu.VMEM((2, n_ctx, BR), F32),
        ],
        compiler_params=_cparams(1),
        name="hgrn2",
    )(gain, *tables, slab, slab, g, slab_c, slab_c, g_c)


def _dft_tables(n):
    lo = 64
    hi = n // lo
    kk = np.arange(n, dtype=np.int64)
    ang_hi = 2.0 * np.pi * ((np.arange(hi, dtype=np.int64)[:, None] * lo * kk[None, :]) % n) / n
    ang_lo = 2.0 * np.pi * ((np.arange(lo, dtype=np.int64)[:, None] * kk[None, :]) % n) / n
    ch, sh = (jnp.asarray(f(ang_hi), F32)[:, None, :] for f in (np.cos, np.sin))
    cl, sl = (jnp.asarray(f(ang_lo), F32)[None, :, :] for f in (np.cos, np.sin))
    scale = 1.0 / math.sqrt(n)
    cos = ((ch * cl - sh * sl) * scale).reshape(n, n)
    sin = ((sh * cl + ch * sl) * scale).reshape(n, n)
    return jnp.concatenate([cos, -sin], axis=1).astype(BF16)


def _channel_dft_tables():
    j = np.arange(BR)
    same = (j[:, None] // FT_DG) == (j[None, :] // FT_DG)
    ang = 2.0 * np.pi * (((j[:, None] % FT_DG) * (j[None, :] % FT_DG)) % FT_DG) / FT_DG
    scale = 1.0 / math.sqrt(FT_DG)
    return (jnp.asarray(np.where(same, np.cos(ang), 0.0) * scale, F32).astype(BF16),
            jnp.asarray(np.where(same, np.sin(ang), 0.0) * scale, F32).astype(BF16))


def _fourier_kernel(cs_ref, bc_ref, bs_ref, u_ref, o_ref):
    u = u_ref[...]
    stacked = jnp.concatenate([_dot(u, bc_ref[...]).astype(BF16), _dot(u, bs_ref[...]).astype(BF16)], axis=0)
    o_ref[...] = _dot(cs_ref[...], stacked).astype(BF16)


def _fourier(cs, bc, bs, slab, *, n):
    batch = slab.shape[0] // n
    return pl.pallas_call(
        _fourier_kernel,
        grid=(batch,),
        in_specs=[
            _resident((n, 2 * n)),
            _resident((BR, BR)),
            _resident((BR, BR)),
            pl.BlockSpec((n, BR), lambda b: (b, COL_U)),
        ],
        out_specs=pl.BlockSpec((n, BR), lambda b: (b, 0)),
        out_shape=jax.ShapeDtypeStruct((batch * n, BR), BF16),
        compiler_params=_cparams(1),
        name="fourier",
    )(cs, bc, bs, slab)


def _merge_kernel(x_ref, mod_ref, ng_ref, ya_ref, yb_ref, yh_ref, yf_ref, sg_ref, wup_ref, wmg_ref, wout_ref, o_ref):
    x = x_ref[...]
    h = _modulated_norm(x, mod_ref, ng_ref).astype(BF16)
    acc = None
    for i, y_ref in enumerate((ya_ref, yb_ref, yh_ref, yf_ref)):
        y = (y_ref[...].astype(F32) * sg_ref[:, i * BR:(i + 1) * BR].astype(F32)).astype(BF16)
        term = jax.nn.sigmoid(_dot(h, wmg_ref[i])) * _dot(y, wup_ref[i])
        acc = term if acc is None else acc + term
    o_ref[...] = x + mod_ref[2:3, :] * _dot(acc.astype(BF16), wout_ref[...])


def _merge(x2d, mod_l, ng, ys, slab, wup, wmg, wout, *, tm, rows_per_mod, mod_row0):
    n = x2d.shape[0]

    def mod_map(t):
        return ((t * tm) // rows_per_mod + mod_row0, 0, 0)

    return pl.pallas_call(
        _merge_kernel,
        grid=(n // tm,),
        in_specs=[
            pl.BlockSpec((tm, D_MODEL), lambda t: (t, 0)),
            pl.BlockSpec((None, 3, D_MODEL), mod_map),
            _resident((1, D_MODEL)),
            *[pl.BlockSpec((tm, BR), lambda t: (t, 0)) for _ in range(4)],
            pl.BlockSpec((tm, 4 * BR), lambda t: (t, COL_SG)),
            _resident((4, BR, D_MODEL)),
            _resident((4, D_MODEL, D_MODEL)),
            _resident((D_MODEL, D_MODEL)),
        ],
        out_specs=pl.BlockSpec((tm, D_MODEL), lambda t: (t, 0)),
        out_shape=jax.ShapeDtypeStruct((n, D_MODEL), F32),
        compiler_params=_cparams(1),
        name="merge",
    )(x2d, mod_l, ng, *ys, slab, wup, wmg, wout)


def _rope_tables(n):
    t = jnp.arange(n)
    row = (t // GRID_W).astype(F32)
    col = (t % GRID_W).astype(F32)
    d_axis = DA_DH // 2
    inv = ROPE_THETA ** (-jnp.arange(0, d_axis, 2, dtype=F32) / d_axis)
    ang = jnp.concatenate([row[:, None] * inv, col[:, None] * inv], axis=-1)
    cos, sin = jnp.cos(ang), jnp.sin(ang)
    cos_h = jnp.concatenate([cos, cos], axis=-1)
    sin_h = jnp.concatenate([-sin, sin], axis=-1)
    return jnp.tile(cos_h, (1, 2 * DA_HEADS)), jnp.tile(sin_h, (1, 2 * DA_HEADS))


def kernel(x, c, ctx, c_ctx, norm_gain, w_mod, b_mod, w_in, da_qk_gain, da_lambda, da_subln_gain,
           na_qk_gain, na_rpb, hg_lb_logits, hg_norm_gain, w_up, w_merge, w_out):
    batch, n, _ = x.shape
    n_ctx = ctx.shape[1]
    assert n == 2048 and n % GRID_W == 0 and batch + 1 <= MOD_ROWS
    assert batch % 2 == 0

    cc = jnp.concatenate([c, c_ctx[None, :], jnp.zeros((MOD_ROWS - batch - 1, D_MODEL), F32)], axis=0)
    mod = _modulation(cc, w_mod, b_mod).reshape(DEPTH, MOD_ROWS, 3, D_MODEL)

    p_lb = jax.nn.softmax(hg_lb_logits.astype(F32), axis=1)
    lower = jnp.cumsum(p_lb, axis=1) - p_lb[:, :1]
    lb_floored = jnp.maximum(lower, LB_FLOOR)
    one_minus_lb = 1.0 - lower
    lv = da_lambda.astype(F32)
    lam_all = jnp.exp(jnp.sum(lv[:, 0] * lv[:, 1], axis=-1)) - jnp.exp(jnp.sum(lv[:, 2] * lv[:, 3], axis=-1))

    cos, sin = _rope_tables(n)
    cs_lat, cs_ctx = _dft_tables(n), _dft_tables(n_ctx)
    bc, bs = _channel_dft_tables()
    bias_all = _na_bias_tables(na_rpb)
    hg_tables = _hgrn_tables()

    xl = x.reshape(batch * n, D_MODEL)
    xc = ctx.reshape(batch * n_ctx, D_MODEL)
    for l in range(DEPTH):
        need_ctx = l < DEPTH - 1
        lam_init = 0.8 - 0.6 * math.exp(-0.3 * l)
        lam = (lam_all[l] + lam_init).reshape(1)
        ng = norm_gain[l].reshape(1, D_MODEL)
        w = w_in[l].astype(BF16)
        gda = jnp.tile(da_qk_gain[l], (1, BR // DA_DH)) * jnp.array([[DA_DH ** -0.5 * LOG2E], [1.0]], F32)
        gna = jnp.tile(na_qk_gain[l], (1, BR // NA_DH)) * jnp.array([[NA_DH ** -0.5 * LOG2E], [1.0]], F32)
        lbp = jnp.stack([lb_floored[0, l], one_minus_lb[0, l], lb_floored[1, l], one_minus_lb[1, l]])
        subln = da_subln_gain[l].reshape(1, DA_DV)
        hgain = jnp.tile(hg_norm_gain[l].reshape(1, HG_D), (1, HG_HEADS))

        slab, g = _inproj(xl, mod[l], ng, w, gda, gna, lbp, cos, sin,
                          tm=512, rows_per_mod=n, mod_row0=0, rope=True)
        slab_c, g_c = _inproj(xc, mod[l], ng, w, gda, gna, lbp, cos[:2 * n_ctx], sin[:2 * n_ctx],
                              tm=2 * n_ctx, rows_per_mod=batch * n_ctx, mod_row0=batch, rope=False)

        ya = _diff_attention_pipelined(lam, subln, slab, slab_c, n=n, n_ctx=n_ctx, tq=DA_QUNIT, lam_init=lam_init)
        yb = _neigh_attention(bias_all, l, slab, slab_c, n=n, n_ctx=n_ctx)
        yh, yh_c = _hgrn(hgain, hg_tables, slab, g, slab_c, g_c, n=n, n_ctx=n_ctx)
        yf = _fourier(cs_lat, bc, bs, slab, n=n)

        wup = w_up[l].astype(BF16)
        wmg = w_merge[l].astype(BF16)
        wout = w_out[l].astype(BF16)
        xl = _merge(xl, mod[l], ng, (ya, yb, yh, yf), slab, wup, wmg, wout,
                    tm=512, rows_per_mod=n, mod_row0=0)
        if need_ctx:
            ya_c = _diff_attention(lam, subln, slab_c, [(slab_c, n_ctx)], nq=n_ctx, tq=n_ctx, lam_init=lam_init)
            yb_c = _ctx_attention(slab_c, n_ctx=n_ctx)
            yf_c = _fourier(cs_ctx, bc, bs, slab_c, n=n_ctx)
            xc = _merge(xc, mod[l], ng, (ya_c, yb_c, yh_c, yf_c), slab_c, wup, wmg, wout,
                        tm=n_ctx, rows_per_mod=batch * n_ctx, mod_row0=batch)
    return xl.reshape(batch, n, D_MODEL)
```
